```python
import math
import jax, jax.numpy as jnp
from jax import lax
import numpy as np

D_MODEL = 1024
BATCH = 16
SEQ = 4096
DEPTH = 4

MEM_LEN = 256
CONV_CH = D_MODEL // 4
CONV_WIDTH = 31
CONV_PAD = (CONV_WIDTH - 1) // 2
DIFF_HEADS = 4
DIFF_W = D_MODEL // 2
DIFF_DV = DIFF_W // DIFF_HEADS
DIFF_DH = DIFF_DV // 2
DIFF_QK_W = DIFF_HEADS * 2 * DIFF_DH
MEM_HEADS = 4
MEM_W = D_MODEL // 4
MEM_DH = MEM_W // MEM_HEADS
MIX_W = CONV_CH + DIFF_W + MEM_W
IN_W = 2 * CONV_CH + 2 * DIFF_QK_W + DIFF_W + MEM_W
D_FF = 4 * D_MODEL
N_BUCKETS = 32
MAX_DISTANCE = 128
Q_BLOCK = 128
ALPHA = (2.0 * DEPTH) ** 0.25
BETA = (8.0 * DEPTH) ** -0.25
LN_EPS = 1e-5

kernel_name = "hybrid_conformer_diffattn_memory_encoder"


def layer_norm(x, g, b):
    xf = x.astype(jnp.float32)
    mu = jnp.mean(xf, axis=-1, keepdims=True)
    var = jnp.mean(jnp.square(xf - mu), axis=-1, keepdims=True)
    y = (xf - mu) * lax.rsqrt(var + LN_EPS) * g.astype(jnp.float32) + b.astype(jnp.float32)
    return y.astype(x.dtype)


def t5_bucket(rel):
    half = N_BUCKETS // 2
    max_exact = half // 2
    ret = (rel > 0).astype(jnp.int32) * half
    n = jnp.abs(rel)
    nf = jnp.maximum(n, 1).astype(jnp.float32)
    large = max_exact + (jnp.log(nf / max_exact) / math.log(MAX_DISTANCE / max_exact)
                         * (half - max_exact)).astype(jnp.int32)
    large = jnp.minimum(large, half - 1)
    return ret + jnp.where(n < max_exact, n, large)


def relative_bias_offsets(rel_bias, seq):
    offsets = jnp.arange(-(seq - 1), seq, dtype=jnp.int32)
    return rel_bias[t5_bucket(offsets)].T.astype(jnp.float32)


def conformer_conv(c_in, conv_w, conv_b, ln_g, ln_b):
    a, gate = jnp.split(c_in, 2, axis=-1)
    u = a * jax.nn.sigmoid(gate)
    y = lax.conv_general_dilated(
        u, conv_w[:, None, :], window_strides=(1,), padding=[(CONV_PAD, CONV_PAD)],
        dimension_numbers=("NWC", "WIO", "NWC"), feature_group_count=CONV_CH) + conv_b
    return jax.nn.silu(layer_norm(y, ln_g, ln_b))


def diff_attention(q, k, v, bias_off, lq1, lk1, lq2, lk2, lam_init, norm_g):
    B, S, _ = q.shape
    H, dh, dv = DIFF_HEADS, DIFF_DH, DIFF_DV
    q = q.reshape(B, S, H, 2, dh)
    k = k.reshape(B, S, H, 2, dh)
    q1 = q[..., 0, :].transpose(0, 2, 1, 3)
    q2 = q[..., 1, :].transpose(0, 2, 1, 3)
    k1 = k[..., 0, :].transpose(0, 2, 1, 3)
    k2 = k[..., 1, :].transpose(0, 2, 1, 3)
    vh = v.reshape(B, S, H, dv).transpose(0, 2, 1, 3)
    lam = (jnp.exp(jnp.sum(lq1.astype(jnp.float32) * lk1.astype(jnp.float32)))
           - jnp.exp(jnp.sum(lq2.astype(jnp.float32) * lk2.astype(jnp.float32)))
           + lam_init)
    scale = dh ** -0.5
    nb = S // Q_BLOCK
    kpos = jnp.arange(S, dtype=jnp.int32)

    def to_blocks(t):
        return t.reshape(B, H, nb, Q_BLOCK, t.shape[-1]).transpose(2, 0, 1, 3, 4)

    def block(args):
        q1b, q2b, start = args
        qpos = start + jnp.arange(Q_BLOCK, dtype=jnp.int32)
        bias = bias_off[:, kpos[None, :] - qpos[:, None] + S - 1]
        s1 = jnp.einsum("bhqd,bhkd->bhqk", q1b, k1).astype(jnp.float32) * scale + bias
        s2 = jnp.einsum("bhqd,bhkd->bhqk", q2b, k2).astype(jnp.float32) * scale + bias
        attn = jax.nn.softmax(s1, axis=-1) - lam * jax.nn.softmax(s2, axis=-1)
        return jnp.einsum("bhqk,bhkd->bhqd", attn.astype(vh.dtype), vh)

    starts = jnp.arange(nb, dtype=jnp.int32) * Q_BLOCK
    o = lax.map(block, (to_blocks(q1), to_blocks(q2), starts))
    o = o.transpose(1, 0, 3, 2, 4).reshape(B, S, H, dv)
    of = o.astype(jnp.float32)
    of = of * lax.rsqrt(jnp.mean(jnp.square(of), axis=-1, keepdims=True) + LN_EPS)
    of = of * norm_g.astype(jnp.float32) * (1.0 - lam_init)
    return of.astype(q.dtype).reshape(B, S, DIFF_W)


def memory_attention(qm, mem, w_mem_kv):
    B, S, _ = qm.shape
    M = mem.shape[1]
    qh = qm.reshape(B, S, MEM_HEADS, MEM_DH)
    km, vm = jnp.split(mem @ w_mem_kv, 2, axis=-1)
    km = km.reshape(B, M, MEM_HEADS, MEM_DH)
    vm = vm.reshape(B, M, MEM_HEADS, MEM_DH)
    s = jnp.einsum("bshd,bmhd->bhsm", qh, km).astype(jnp.float32) * (MEM_DH ** -0.5)
    p = jax.nn.softmax(s, axis=-1).astype(vm.dtype)
    return jnp.einsum("bhsm,bmhd->bshd", p, vm).reshape(B, S, MEM_W)


def hybrid_mixer(h, mem, bias_off, w_in, b_in, conv_w, conv_b, conv_ln_g, conv_ln_b,
                 lq1, lk1, lq2, lk2, lam_init, diff_norm_g, w_mem_kv, w_out, b_out):
    proj = h @ w_in + b_in
    s0 = 2 * CONV_CH
    s1 = s0 + DIFF_QK_W
    s2 = s1 + DIFF_QK_W
    s3 = s2 + DIFF_W
    c_in, q, k, v, qm = jnp.split(proj, [s0, s1, s2, s3], axis=-1)
    conv_out = conformer_conv(c_in, conv_w, conv_b, conv_ln_g, conv_ln_b)
    diff_out = diff_attention(q, k, v, bias_off, lq1, lk1, lq2, lk2, lam_init, diff_norm_g)
    mem_out = memory_attention(qm, mem, w_mem_kv)
    mixed = jnp.concatenate([conv_out, diff_out, mem_out], axis=-1)
    return mixed @ w_out + b_out


def setup_inputs(seed: int = 0) -> dict:
    key = jax.random.key(seed)
    ks = jax.random.split(key, 32)
    L, D = DEPTH, D_MODEL
    nrm = jax.random.normal
    f32 = jnp.float32
    return {
        "x": nrm(ks[0], (BATCH, SEQ, D), f32),
        "mem": nrm(ks[1], (BATCH, MEM_LEN, D), f32),
        "emb_ln_g": 1.0 + 0.02 * nrm(ks[2], (D,), f32),
        "emb_ln_b": 0.02 * nrm(ks[3], (D,), f32),
        "rel_bias": 0.5 * nrm(ks[4], (N_BUCKETS, DIFF_HEADS), f32),
        "w_in": nrm(ks[5], (L, D, IN_W), f32) * D ** -0.5,
        "b_in": 0.02 * nrm(ks[6], (L, IN_W), f32),
        "conv_w": nrm(ks[7], (L, CONV_WIDTH, CONV_CH), f32) * CONV_WIDTH ** -0.5,
        "conv_b": 0.02 * nrm(ks[8], (L, CONV_CH), f32),
        "conv_ln_g": 1.0 + 0.02 * nrm(ks[9], (L, CONV_CH), f32),
        "conv_ln_b": 0.02 * nrm(ks[10], (L, CONV_CH), f32),
        "lambda_q1": 0.1 * nrm(ks[11], (L, DIFF_DH), f32),
        "lambda_k1": 0.1 * nrm(ks[12], (L, DIFF_DH), f32),
        "lambda_q2": 0.1 * nrm(ks[13], (L, DIFF_DH), f32),
        "lambda_k2": 0.1 * nrm(ks[14], (L, DIFF_DH), f32),
        "diff_norm_g": 1.0 + 0.02 * nrm(ks[15], (L, DIFF_DV), f32),
        "w_mem_kv": nrm(ks[16], (L, D, 2 * MEM_W), f32) * D ** -0.5,
        "w_out": nrm(ks[17], (L, MIX_W, D), f32) * (MIX_W ** -0.5) * BETA,
        "b_out": 0.02 * nrm(ks[18], (L, D), f32),
        "ln1_g": 1.0 + 0.02 * nrm(ks[19], (L, D), f32),
        "ln1_b": 0.02 * nrm(ks[20], (L, D), f32),
        "w_up": nrm(ks[21], (L, D, D_FF), f32) * D ** -0.5,
        "w_down": nrm(ks[22], (L, D_FF, D), f32) * (D_FF ** -0.5) * BETA,
        "ln2_g": 1.0 + 0.02 * nrm(ks[23], (L, D), f32),
        "ln2_b": 0.02 * nrm(ks[24], (L, D), f32),
    }


def reference(x, mem, emb_ln_g, emb_ln_b, rel_bias, w_in, b_in, conv_w, conv_b,
              conv_ln_g, conv_ln_b, lambda_q1, lambda_k1, lambda_q2, lambda_k2,
              diff_norm_g, w_mem_kv, w_out, b_out, ln1_g, ln1_b, w_up, w_down,
              ln2_g, ln2_b):
    S = x.shape[1]
    x = layer_norm(x, emb_ln_g, emb_ln_b)
    bias_off = relative_bias_offsets(rel_bias, S)
    for l in range(DEPTH):
        lam_init = 0.8 - 0.6 * math.exp(-0.3 * l)
        mixed = hybrid_mixer(x, mem, bias_off, w_in[l], b_in[l], conv_w[l], conv_b[l],
                             conv_ln_g[l], conv_ln_b[l], lambda_q1[l], lambda_k1[l],
                             lambda_q2[l], lambda_k2[l], lam_init, diff_norm_g[l],
                             w_mem_kv[l], w_out[l], b_out[l])
        x = layer_norm(ALPHA * x + mixed, ln1_g[l], ln1_b[l])
        ff = jnp.square(jax.nn.relu(x @ w_up[l])) @ w_down[l]
        x = layer_norm(ALPHA * x + ff, ln2_g[l], ln2_b[l])
    return x
```

```python
import functools
import math

import jax
import jax.numpy as jnp
from jax import lax
from jax.experimental import pallas as pl
from jax.experimental.pallas import tpu as pltpu

F32 = jnp.float32
BF16 = jnp.bfloat16

DEPTH = 4
CONV_WIDTH = 31
CONV_PAD = (CONV_WIDTH - 1) // 2
DIFF_HEADS = 4
MEM_HEADS = 4
N_BUCKETS = 32
MAX_DISTANCE = 128
ALPHA = (2.0 * DEPTH) ** 0.25
LN_EPS = 1e-5

LANES = 128
SUBLANES = 8
BF16_ROWS = 16
MXU_DIM = 256
VMEM_LIMIT = 56 * 1024 * 1024

ATT_TILE = MXU_DIM
HALO = 16

_NT = (((1,), (1,)), ((), ()))


def _cparams(*sem):
    return pltpu.CompilerParams(dimension_semantics=sem, vmem_limit_bytes=VMEM_LIMIT)


def _layer_norm(z, g, b):
    mu = jnp.mean(z, axis=-1, keepdims=True)
    zc = z - mu
    var = jnp.mean(zc * zc, axis=-1, keepdims=True)
    return zc * lax.rsqrt(var + LN_EPS) * g + b


def _ln_kernel(x_ref, g_ref, b_ref, o_ref):
    o_ref[...] = _layer_norm(x_ref[...], g_ref[...], b_ref[...])


def _embed_ln(x2d, g, b, tm=1024):
    m, d = x2d.shape
    return pl.pallas_call(
        _ln_kernel,
        grid=(m // tm,),
        in_specs=[pl.BlockSpec((tm, d), lambda i: (i, 0)),
                  pl.BlockSpec((1, d), lambda i: (0, 0)),
                  pl.BlockSpec((1, d), lambda i: (0, 0))],
        out_specs=pl.BlockSpec((tm, d), lambda i: (i, 0)),
        out_shape=jax.ShapeDtypeStruct((m, d), F32),
        compiler_params=_cparams("parallel"),
        name="embed_ln",
    )(x2d, g.reshape(1, d), b.reshape(1, d))


def _inproj_kernel(x_ref, wc_ref, wk_ref, wqt_ref, wvt_ref, wm_ref,
                   bc_ref, bk_ref, bqt_ref, bvt_ref, bm_ref,
                   c_ref, k_ref, qt_ref, vt_ref, qm_ref, *, q_scale, qm_scale):
    xb = x_ref[0].astype(BF16)
    c_ref[0] = jnp.dot(xb, wc_ref[...], preferred_element_type=F32) + bc_ref[...]
    k_ref[0] = (jnp.dot(xb, wk_ref[...], preferred_element_type=F32) + bk_ref[...]).astype(BF16)
    qt = lax.dot_general(wqt_ref[...], xb, _NT, preferred_element_type=F32) + bqt_ref[...]
    qt_ref[0] = (qt * q_scale).astype(BF16)
    vt = lax.dot_general(wvt_ref[...], xb, _NT, preferred_element_type=F32) + bvt_ref[...]
    vt_ref[0] = vt.astype(BF16)
    qm = jnp.dot(xb, wm_ref[...], preferred_element_type=F32) + bm_ref[...]
    qm_ref[0] = (qm * qm_scale).astype(BF16)


def _inproj(x, w_in, b_in, dims, tm=512):
    bsz, s, d = x.shape
    conv_ch, qk_w, diff_w, mem_w = dims
    s0 = 2 * conv_ch
    s1 = s0 + qk_w
    s2 = s1 + qk_w
    s3 = s2 + diff_w
    wb = w_in.astype(BF16)
    wc, wq, wk, wv, wm = wb[:, :s0], wb[:, s0:s1], wb[:, s1:s2], wb[:, s2:s3], wb[:, s3:]
    bc, bq, bk, bv, bm = b_in[:s0], b_in[s0:s1], b_in[s1:s2], b_in[s2:s3], b_in[s3:]
    dh = qk_w // (2 * DIFF_HEADS)
    mem_dh = mem_w // MEM_HEADS
    kern = functools.partial(_inproj_kernel, q_scale=dh ** -0.5, qm_scale=mem_dh ** -0.5)
    const = lambda shape: pl.BlockSpec(shape, lambda b, i: (0, 0))
    return pl.pallas_call(
        kern,
        grid=(bsz, s // tm),
        in_specs=[pl.BlockSpec((1, tm, d), lambda b, i: (b, i, 0)),
                  const((d, s0)), const((d, qk_w)), const((qk_w, d)), const((diff_w, d)), const((d, mem_w)),
                  const((1, s0)), const((1, qk_w)), const((qk_w, 1)), const((diff_w, 1)), const((1, mem_w))],
        out_specs=[pl.BlockSpec((1, tm, s0), lambda b, i: (b, i, 0)),
                   pl.BlockSpec((1, tm, qk_w), lambda b, i: (b, i, 0)),
                   pl.BlockSpec((1, qk_w, tm), lambda b, i: (b, 0, i)),
                   pl.BlockSpec((1, diff_w, tm), lambda b, i: (b, 0, i)),
                   pl.BlockSpec((1, tm, mem_w), lambda b, i: (b, i, 0))],
        out_shape=[jax.ShapeDtypeStruct((bsz, s, s0), F32),
                   jax.ShapeDtypeStruct((bsz, s, qk_w), BF16),
                   jax.ShapeDtypeStruct((bsz, qk_w, s), BF16),
                   jax.ShapeDtypeStruct((bsz, diff_w, s), BF16),
                   jax.ShapeDtypeStruct((bsz, s, mem_w), BF16)],
        compiler_params=_cparams("parallel", "parallel"),
        name="inproj",
    )(x, wc, wk, wq.T, wv.T, wm,
      bc.reshape(1, -1), bk.reshape(1, -1), bq.reshape(-1, 1), bv.reshape(-1, 1), bm.reshape(1, -1))


def _conv_kernel(prev_ref, cur_ref, next_ref, w_ref, cb_ref, g_ref, b_ref, o_ref, u_ref, *, ts, ch, rows):
    i = pl.program_id(1)
    n = pl.num_programs(1)

    def glu(c):
        return c[:, :ch] * jax.nn.sigmoid(c[:, ch:])

    u_ref[0:HALO, :] = jnp.where(i > 0, glu(prev_ref[0]), 0.0)
    u_ref[HALO:HALO + ts, :] = glu(cur_ref[0])
    u_ref[HALO + ts:, :] = jnp.where(i < n - 1, glu(next_ref[0]), 0.0)

    w = w_ref[...]
    for r in range(0, ts, rows):
        acc = jnp.zeros((rows, ch), F32)
        for t in range(CONV_WIDTH):
            start = HALO - CONV_PAD + r + t
            acc = acc + u_ref[start:start + rows, :] * w[t:t + 1, :]
        y = _layer_norm(acc + cb_ref[...], g_ref[...], b_ref[...])
        o_ref[0, r:r + rows, :] = (y * jax.nn.sigmoid(y)).astype(BF16)


def _conformer_conv(c_in, conv_w, conv_b, ln_g, ln_b, ts=256, rows=64):
    bsz, s, two_ch = c_in.shape
    ch = two_ch // 2
    hb = ts // HALO
    nh = s // HALO
    kern = functools.partial(_conv_kernel, ts=ts, ch=ch, rows=rows)
    vec = lambda: pl.BlockSpec((1, ch), lambda b, i: (0, 0))
    return pl.pallas_call(
        kern,
        grid=(bsz, s // ts),
        in_specs=[pl.BlockSpec((1, HALO, two_ch), lambda b, i: (b, jnp.maximum(i * hb - 1, 0), 0)),
                  pl.BlockSpec((1, ts, two_ch), lambda b, i: (b, i, 0)),
                  pl.BlockSpec((1, HALO, two_ch), lambda b, i: (b, jnp.minimum((i + 1) * hb, nh - 1), 0)),
                  pl.BlockSpec((CONV_WIDTH, ch), lambda b, i: (0, 0)),
                  vec(), vec(), vec()],
        out_specs=pl.BlockSpec((1, ts, ch), lambda b, i: (b, i, 0)),
        out_shape=jax.ShapeDtypeStruct((bsz, s, ch), BF16),
        scratch_shapes=[pltpu.VMEM((ts + 2 * HALO, ch), F32)],
        compiler_params=_cparams("parallel", "parallel"),
        name="conformer_conv",
    )(c_in, c_in, c_in, conv_w, conv_b.reshape(1, ch), ln_g.reshape(1, ch), ln_b.reshape(1, ch))


def _memkv_kernel(mem_ref, wkt_ref, wv_ref, kbd_ref, vbd_ref, *, heads):
    mb = mem_ref[0].astype(BF16)
    kmt = lax.dot_general(wkt_ref[...], mb, _NT, preferred_element_type=F32)
    vm = jnp.dot(mb, wv_ref[...], preferred_element_type=F32)
    w, m = kmt.shape
    dh = w // heads
    row_head = lax.broadcasted_iota(jnp.int32, (w, m), 0) // dh
    col_head = lax.broadcasted_iota(jnp.int32, (m, w), 1) // dh
    for h in range(heads):
        kbd_ref[0, :, h * m:(h + 1) * m] = jnp.where(row_head == h, kmt, 0.0).astype(BF16)
        vbd_ref[0, h * m:(h + 1) * m, :] = jnp.where(col_head == h, vm, 0.0).astype(BF16)


def _memkv(mem, w_mem_kv, mem_w):
    bsz, m, d = mem.shape
    wb = w_mem_kv.astype(BF16)
    wkt, wv = wb[:, :mem_w].T, wb[:, mem_w:]
    kern = functools.partial(_memkv_kernel, heads=MEM_HEADS)
    return pl.pallas_call(
        kern,
        grid=(bsz,),
        in_specs=[pl.BlockSpec((1, m, d), lambda b: (b, 0, 0)),
                  pl.BlockSpec((mem_w, d), lambda b: (0, 0)),
                  pl.BlockSpec((d, mem_w), lambda b: (0, 0))],
        out_specs=[pl.BlockSpec((1, mem_w, MEM_HEADS * m), lambda b: (b, 0, 0)),
                   pl.BlockSpec((1, MEM_HEADS * m, mem_w), lambda b: (b, 0, 0))],
        out_shape=[jax.ShapeDtypeStruct((bsz, mem_w, MEM_HEADS * m), BF16),
                   jax.ShapeDtypeStruct((bsz, MEM_HEADS * m, mem_w), BF16)],
        compiler_params=_cparams("parallel"),
        name="mem_kv",
    )(mem, wkt, wv)


def _memattn_kernel(qm_ref, kbd_ref, vbd_ref, o_ref, *, heads):
    s = jnp.dot(qm_ref[0], kbd_ref[0], preferred_element_type=F32)
    m = s.shape[1] // heads
    w = o_ref.shape[2]
    dh = w // heads
    lane_head = lax.broadcasted_iota(jnp.int32, (1, w), 1) // dh
    ps = []
    scale = jnp.zeros((s.shape[0], w), F32)
    for h in range(heads):
        sh = s[:, h * m:(h + 1) * m]
        p = jnp.exp(sh - jnp.max(sh, axis=-1, keepdims=True))
        scale = jnp.where(lane_head == h, 1.0 / jnp.sum(p, axis=-1, keepdims=True), scale)
        ps.append(p.astype(BF16))
    o = jnp.dot(jnp.concatenate(ps, axis=1), vbd_ref[0], preferred_element_type=F32)
    o_ref[0] = (o * scale).astype(BF16)


def _memattn(qm, kbd, vbd, tq=512):
    bsz, s, w = qm.shape
    hm = kbd.shape[2]
    kern = functools.partial(_memattn_kernel, heads=MEM_HEADS)
    return pl.pallas_call(
        kern,
        grid=(bsz, s // tq),
        in_specs=[pl.BlockSpec((1, tq, w), lambda b, i: (b, i, 0)),
                  pl.BlockSpec((1, w, hm), lambda b, i: (b, 0, 0)),
                  pl.BlockSpec((1, hm, w), lambda b, i: (b, 0, 0))],
        out_specs=pl.BlockSpec((1, tq, w), lambda b, i: (b, i, 0)),
        out_shape=jax.ShapeDtypeStruct((bsz, s, w), BF16),
        compiler_params=_cparams("parallel", "parallel"),
        name="mem_attn",
    )(qm, kbd, vbd)


def _t5_bucket(rel):
    half = N_BUCKETS // 2
    max_exact = half // 2
    ret = (rel > 0).astype(jnp.int32) * half
    n = jnp.abs(rel)
    nf = jnp.maximum(n, 1).astype(jnp.float32)
    large = max_exact + (jnp.log(nf / max_exact) / math.log(MAX_DISTANCE / max_exact)
                         * (half - max_exact)).astype(jnp.int32)
    large = jnp.minimum(large, half - 1)
    return ret + jnp.where(n < max_exact, n, large)


def _bias_tile_kernel(relb_ref, idx_ref, o_ref):
    h = pl.program_id(0)
    idx = idx_ref[0]
    acc = jnp.zeros(idx.shape, F32)
    for bkt in range(N_BUCKETS):
        acc = jnp.where(idx == bkt, relb_ref[bkt, h], acc)
    o_ref[0, 0] = acc


def _bias_tiles(rel_bias, s):
    t = ATT_TILE
    sat = (N_BUCKETS // 4) * (MAX_DISTANCE / (N_BUCKETS // 4)) ** ((N_BUCKETS // 2 - 1 - N_BUCKETS // 4) / (N_BUCKETS // 4))
    assert t + 1 > sat + 1, "attention tile too small for the saturated-bias shortcut"
    bucket = _t5_bucket(jnp.arange(-(s - 1), s, dtype=jnp.int32))
    kk = jnp.arange(t, dtype=jnp.int32)[:, None]
    qq = jnp.arange(t, dtype=jnp.int32)[None, :]
    idx = jnp.stack([bucket[(d - 1) * t + kk - qq + s - 1] for d in range(3)])
    far = jnp.stack([bucket[0], bucket[2 * s - 2]])
    tiles = pl.pallas_call(
        _bias_tile_kernel,
        grid=(DIFF_HEADS, 3),
        in_specs=[pl.BlockSpec(memory_space=pltpu.SMEM),
                  pl.BlockSpec((1, t, t), lambda h, d: (d, 0, 0))],
        out_specs=pl.BlockSpec((1, 1, t, t), lambda h, d: (h, d, 0, 0)),
        out_shape=jax.ShapeDtypeStruct((DIFF_HEADS, 3, t, t), F32),
        compiler_params=_cparams("parallel", "parallel"),
        name="bias_tiles",
    )(rel_bias, idx)
    return tiles, far


def _diffattn_kernel(far_ref, relb_ref, lami_ref, qt_ref, k_ref, vt_ref, bias_ref, g_ref,
                     lq1_ref, lk1_ref, lq2_ref, lk2_ref, o_ref, vaug_ref, s_ref, p_ref, *, dh, dv, nk):
    t = ATT_TILE
    h = pl.program_id(1)
    qi = pl.program_id(2)

    @pl.when(qi == 0)
    def _():
        vaug_ref[0:dv, :] = vt_ref[0]
        vaug_ref[dv:, :] = jnp.ones((BF16_ROWS, vaug_ref.shape[1]), BF16)

    c_left = relb_ref[far_ref[0], h]
    c_right = relb_ref[far_ref[1], h]
    qt = qt_ref[0]
    row = lax.broadcasted_iota(jnp.int32, qt.shape, 0)
    lo = jnp.maximum(qi - 1, 0)
    hi = jnp.minimum(qi + 2, nk)

    def attend(qtm):
        def scores(j):
            kj = k_ref[0, pl.ds(pl.multiple_of(j * t, t), t), :]
            return jnp.dot(kj, qtm, preferred_element_type=F32)

        def colmax(s):
            return jnp.max(s.reshape(t // SUBLANES, SUBLANES, t), axis=0)

        def far_body(c):
            def body(j, m8):
                s = scores(j)
                s_ref[j] = s
                return jnp.maximum(m8, colmax(s) + c)
            return body

        def near_body(j, m8):
            s = scores(j) + bias_ref[0, j - qi + 1]
            s_ref[j] = s
            return jnp.maximum(m8, colmax(s))

        m8 = jnp.full((SUBLANES, t), -jnp.inf, F32)
        m8 = lax.fori_loop(0, lo, far_body(c_left), m8)
        m8 = lax.fori_loop(lo, hi, near_body, m8)
        m8 = lax.fori_loop(hi, nk, far_body(c_right), m8)
        m = jnp.max(m8, axis=0, keepdims=True)

        def p_body(j, carry):
            c = jnp.where(j < qi - 1, c_left, jnp.where(j > qi + 1, c_right, 0.0))
            p = jnp.exp(s_ref[j] - (m - c))
            p_ref[pl.ds(pl.multiple_of(j * t, t), t), :] = p.astype(BF16)
            return carry

        lax.fori_loop(0, nk, p_body, 0)
        acc = jnp.dot(vaug_ref[...], p_ref[...], preferred_element_type=F32)
        return acc[0:dv] / acc[dv:dv + 1]

    o1 = attend(jnp.where(row < dh, qt, jnp.zeros_like(qt)))
    o2 = attend(jnp.where(row >= dh, qt, jnp.zeros_like(qt)))

    lam_init = lami_ref[0]
    e1 = jnp.exp(jnp.sum(lq1_ref[...] * lk1_ref[...], axis=-1, keepdims=True))
    e2 = jnp.exp(jnp.sum(lq2_ref[...] * lk2_ref[...], axis=-1, keepdims=True))
    lam = e1 - e2 + lam_init
    ot = o1 - lam * o2
    ms = jnp.mean(ot * ot, axis=0, keepdims=True)
    ot = ot * lax.rsqrt(ms + LN_EPS) * g_ref[...] * (1.0 - lam_init)
    o_ref[0] = ot.T.astype(BF16)


def _diffattn(qt, k, vt, bias_tiles, far, rel_bias, lam_init, norm_g, lq1, lk1, lq2, lk2):
    bsz, s, qk_w = k.shape
    diff_w = vt.shape[1]
    dv = diff_w // DIFF_HEADS
    dh = qk_w // (2 * DIFF_HEADS)
    t = ATT_TILE
    nk = s // t
    kern = functools.partial(_diffattn_kernel, dh=dh, dv=dv, nk=nk)
    smem = pl.BlockSpec(memory_space=pltpu.SMEM)
    lvec = lambda: pl.BlockSpec((1, dh), lambda b, h, i: (0, 0))
    return pl.pallas_call(
        kern,
        grid=(bsz, DIFF_HEADS, nk),
        in_specs=[smem, smem, smem,
                  pl.BlockSpec((1, 2 * dh, t), lambda b, h, i: (b, h, i)),
                  pl.BlockSpec((1, s, 2 * dh), lambda b, h, i: (b, 0, h)),
                  pl.BlockSpec((1, dv, s), lambda b, h, i: (b, h, 0)),
                  pl.BlockSpec((1, 3, t, t), lambda b, h, i: (h, 0, 0, 0)),
                  pl.BlockSpec((dv, 1), lambda b, h, i: (0, 0)),
                  lvec(), lvec(), lvec(), lvec()],
        out_specs=pl.BlockSpec((1, t, dv), lambda b, h, i: (b, i, h)),
        out_shape=jax.ShapeDtypeStruct((bsz, s, diff_w), BF16),
        scratch_shapes=[pltpu.VMEM((dv + BF16_ROWS, s), BF16),
                        pltpu.VMEM((nk, t, t), F32),
                        pltpu.VMEM((s, t), BF16)],
        compiler_params=_cparams("parallel", "parallel", "arbitrary"),
        name="diff_attn",
    )(far, rel_bias, jnp.full((1,), lam_init, F32), qt, k, vt, bias_tiles, norm_g.reshape(dv, 1),
      lq1.reshape(1, dh), lk1.reshape(1, dh), lq2.reshape(1, dh), lk2.reshape(1, dh))


def _outproj_kernel(x_ref, c_ref, d_ref, m_ref, wc_ref, wd_ref, wm_ref, b_ref, g_ref, beta_ref, o_ref):
    y = jnp.dot(c_ref[...], wc_ref[...], preferred_element_type=F32)
    y = y + jnp.dot(d_ref[...], wd_ref[...], preferred_element_type=F32)
    y = y + jnp.dot(m_ref[...], wm_ref[...], preferred_element_type=F32)
    z = ALPHA * x_ref[...] + (y + b_ref[...])
    o_ref[...] = _layer_norm(z, g_ref[...], beta_ref[...])


def _outproj_ln(x2d, conv_out, diff_out, mem_out, w_out, b_out, g, beta, tm=512):
    m, d = x2d.shape
    cw, dw, mw = conv_out.shape[1], diff_out.shape[1], mem_out.shape[1]
    wb = w_out.astype(BF16)
    rows = lambda w: pl.BlockSpec((tm, w), lambda i: (i, 0))
    const = lambda shape: pl.BlockSpec(shape, lambda i: (0, 0))
    return pl.pallas_call(
        _outproj_kernel,
        grid=(m // tm,),
        in_specs=[rows(d), rows(cw), rows(dw), rows(mw),
                  const((cw, d)), const((dw, d)), const((mw, d)),
                  const((1, d)), const((1, d)), const((1, d))],
        out_specs=rows(d),
        out_shape=jax.ShapeDtypeStruct((m, d), F32),
        compiler_params=_cparams("parallel"),
        name="outproj_ln",
    )(x2d, conv_out, diff_out, mem_out, wb[:cw], wb[cw:cw + dw], wb[cw + dw:],
      b_out.reshape(1, d), g.reshape(1, d), beta.reshape(1, d))


def _mlp_kernel(x_ref, wu_ref, wd_ref, g_ref, beta_ref, o_ref, *, chunk):
    x = x_ref[...]
    xb = x.astype(BF16)
    ff = jnp.zeros(x.shape, F32)
    for c in range(0, wu_ref.shape[1], chunk):
        hcol = jnp.maximum(jnp.dot(xb, wu_ref[:, c:c + chunk], preferred_element_type=F32), 0.0)
        ff = ff + jnp.dot((hcol * hcol).astype(BF16), wd_ref[c:c + chunk, :], preferred_element_type=F32)
    o_ref[...] = _layer_norm(ALPHA * x + ff, g_ref[...], beta_ref[...])


def _mlp_ln(x2d, w_up, w_down, g, beta, tm=512, chunk=1024):
    m, d = x2d.shape
    dff = w_up.shape[1]
    kern = functools.partial(_mlp_kernel, chunk=chunk)
    const = lambda shape: pl.BlockSpec(shape, lambda i: (0, 0))
    return pl.pallas_call(
        kern,
        grid=(m // tm,),
        in_specs=[pl.BlockSpec((tm, d), lambda i: (i, 0)),
                  const((d, dff)), const((dff, d)), const((1, d)), const((1, d))],
        out_specs=pl.BlockSpec((tm, d), lambda i: (i, 0)),
        out_shape=jax.ShapeDtypeStruct((m, d), F32),
        compiler_params=_cparams("parallel"),
        name="mlp_ln",
    )(x2d, w_up.astype(BF16), w_down.astype(BF16), g.reshape(1, d), beta.reshape(1, d))


def kernel(x, mem, emb_ln_g, emb_ln_b, rel_bias, w_in, b_in, conv_w, conv_b, conv_ln_g, conv_ln_b,
           lambda_q1, lambda_k1, lambda_q2, lambda_k2, diff_norm_g, w_mem_kv, w_out, b_out,
           ln1_g, ln1_b, w_up, w_down, ln2_g, ln2_b):
    bsz, s, d = x.shape
    conv_ch = conv_w.shape[2]
    dh = lambda_q1.shape[1]
    qk_w = DIFF_HEADS * 2 * dh
    diff_w = DIFF_HEADS * diff_norm_g.shape[1]
    mem_w = w_mem_kv.shape[2] // 2
    dims = (conv_ch, qk_w, diff_w, mem_w)
    assert s % ATT_TILE == 0 and 2 * dh == LANES and diff_norm_g.shape[1] == LANES

    xs = _embed_ln(x.reshape(bsz * s, d), emb_ln_g, emb_ln_b)
    bias_tiles, far = _bias_tiles(rel_bias, s)
    for l in range(DEPTH):
        lam_init = 0.8 - 0.6 * math.exp(-0.3 * l)
        c_in, k, qt, vt, qm = _inproj(xs.reshape(bsz, s, d), w_in[l], b_in[l], dims)
        conv_out = _conformer_conv(c_in, conv_w[l], conv_b[l], conv_ln_g[l], conv_ln_b[l])
        diff_out = _diffattn(qt, k, vt, bias_tiles, far, rel_bias, lam_init, diff_norm_g[l],
                             lambda_q1[l], lambda_k1[l], lambda_q2[l], lambda_k2[l])
        kbd, vbd = _memkv(mem, w_mem_kv[l], mem_w)
        mem_out = _memattn(qm, kbd, vbd)
        xs = _outproj_ln(xs, conv_out.reshape(bsz * s, -1), diff_out.reshape(bsz * s, -1),
                         mem_out.reshape(bsz * s, -1), w_out[l], b_out[l], ln1_g[l], ln1_b[l])
        xs = _mlp_ln(xs, w_up[l], w_down[l], ln2_g[l], ln2_b[l])
    return xs.reshape(bsz, s, d)
```

```python
import functools
import math

import jax
import jax.numpy as jnp
from jax import lax
from jax.experimental import pallas as pl
from jax.experimental.pallas import tpu as pltpu

F32 = jnp.float32
BF16 = jnp.bfloat16

DEPTH = 4
CONV_WIDTH = 31
CONV_PAD = (CONV_WIDTH - 1) // 2
DIFF_HEADS = 4
MEM_HEADS = 4
N_BUCKETS = 32
MAX_DISTANCE = 128
ALPHA = (2.0 * DEPTH) ** 0.25
LN_EPS = 1e-5
LOG2E = math.log2(math.e)

LANES = 128
SUBLANES = 8
BF16_ROWS = 16
MXU_DIM = 256
VMEM_LIMIT = 56 * 1024 * 1024

ATT_TILE = MXU_DIM
HALO = 16

_NT = (((1,), (1,)), ((), ()))


def _cparams(*sem):
    return pltpu.CompilerParams(dimension_semantics=sem, vmem_limit_bytes=VMEM_LIMIT)


def _layer_norm(z, g, b):
    mu = jnp.mean(z, axis=-1, keepdims=True)
    zc = z - mu
    var = jnp.mean(zc * zc, axis=-1, keepdims=True)
    return zc * lax.rsqrt(var + LN_EPS) * g + b


def _ln_kernel(x_ref, g_ref, b_ref, o_ref):
    o_ref[...] = _layer_norm(x_ref[...], g_ref[...], b_ref[...])


def _embed_ln(x2d, g, b, tm=1024):
    m, d = x2d.shape
    return pl.pallas_call(
        _ln_kernel,
        grid=(m // tm,),
        in_specs=[pl.BlockSpec((tm, d), lambda i: (i, 0)),
                  pl.BlockSpec((1, d), lambda i: (0, 0)),
                  pl.BlockSpec((1, d), lambda i: (0, 0))],
        out_specs=pl.BlockSpec((tm, d), lambda i: (i, 0)),
        out_shape=jax.ShapeDtypeStruct((m, d), F32),
        compiler_params=_cparams("parallel"),
        name="embed_ln",
    )(x2d, g.reshape(1, d), b.reshape(1, d))


def _inproj_kernel(x_ref, wc_ref, wk_ref, wqt_ref, wvt_ref, wm_ref,
                   bc_ref, bk_ref, bqt_ref, bvt_ref, bm_ref,
                   c_ref, k_ref, qt_ref, vt_ref, qm_ref, *, q_scale, qm_scale):
    xb = x_ref[0].astype(BF16)
    c_ref[0] = jnp.dot(xb, wc_ref[...], preferred_element_type=F32) + bc_ref[...]
    k_ref[0] = (jnp.dot(xb, wk_ref[...], preferred_element_type=F32) + bk_ref[...]).astype(BF16)
    qt = lax.dot_general(wqt_ref[...], xb, _NT, preferred_element_type=F32) + bqt_ref[...]
    qt_ref[0] = (qt * q_scale).astype(BF16)
    vt = lax.dot_general(wvt_ref[...], xb, _NT, preferred_element_type=F32) + bvt_ref[...]
    vt_ref[0] = vt.astype(BF16)
    qm = jnp.dot(xb, wm_ref[...], preferred_element_type=F32) + bm_ref[...]
    qm_ref[0] = (qm * qm_scale).astype(BF16)


def _inproj(x, w_in, b_in, dims, tm=512):
    bsz, s, d = x.shape
    conv_ch, qk_w, diff_w, mem_w = dims
    s0 = 2 * conv_ch
    s1 = s0 + qk_w
    s2 = s1 + qk_w
    s3 = s2 + diff_w
    wb = w_in.astype(BF16)
    wc, wq, wk, wv, wm = wb[:, :s0], wb[:, s0:s1], wb[:, s1:s2], wb[:, s2:s3], wb[:, s3:]
    bc, bq, bk, bv, bm = b_in[:s0], b_in[s0:s1], b_in[s1:s2], b_in[s2:s3], b_in[s3:]
    dh = qk_w // (2 * DIFF_HEADS)
    mem_dh = mem_w // MEM_HEADS
    kern = functools.partial(_inproj_kernel, q_scale=dh ** -0.5 * LOG2E, qm_scale=mem_dh ** -0.5)
    const = lambda shape: pl.BlockSpec(shape, lambda b, i: (0, 0))
    return pl.pallas_call(
        kern,
        grid=(bsz, s // tm),
        in_specs=[pl.BlockSpec((1, tm, d), lambda b, i: (b, i, 0)),
                  const((d, s0)), const((d, qk_w)), const((qk_w, d)), const((diff_w, d)), const((d, mem_w)),
                  const((1, s0)), const((1, qk_w)), const((qk_w, 1)), const((diff_w, 1)), const((1, mem_w))],
        out_specs=[pl.BlockSpec((1, tm, s0), lambda b, i: (b, i, 0)),
                   pl.BlockSpec((1, tm, qk_w), lambda b, i: (b, i, 0)),
                   pl.BlockSpec((1, qk_w, tm), lambda b, i: (b, 0, i)),
                   pl.BlockSpec((1, diff_w, tm), lambda b, i: (b, 0, i)),
                   pl.BlockSpec((1, tm, mem_w), lambda b, i: (b, i, 0))],
        out_shape=[jax.ShapeDtypeStruct((bsz, s, s0), F32),
                   jax.ShapeDtypeStruct((bsz, s, qk_w), BF16),
                   jax.ShapeDtypeStruct((bsz, qk_w, s), BF16),
                   jax.ShapeDtypeStruct((bsz, diff_w, s), BF16),
                   jax.ShapeDtypeStruct((bsz, s, mem_w), BF16)],
        compiler_params=_cparams("parallel", "parallel"),
        name="inproj",
    )(x, wc, wk, wq.T, wv.T, wm,
      bc.reshape(1, -1), bk.reshape(1, -1), bq.reshape(-1, 1), bv.reshape(-1, 1), bm.reshape(1, -1))


def _conv_kernel(prev_ref, cur_ref, next_ref, w_ref, cb_ref, g_ref, b_ref, o_ref, u_ref, *, ts, ch, rows):
    i = pl.program_id(1)
    n = pl.num_programs(1)

    def glu(c):
        return c[:, :ch] * jax.nn.sigmoid(c[:, ch:])

    u_ref[0:HALO, :] = jnp.where(i > 0, glu(prev_ref[0]), 0.0)
    u_ref[HALO:HALO + ts, :] = glu(cur_ref[0])
    u_ref[HALO + ts:, :] = jnp.where(i < n - 1, glu(next_ref[0]), 0.0)

    w = w_ref[...]
    for r in range(0, ts, rows):
        acc = jnp.zeros((rows, ch), F32)
        for t in range(CONV_WIDTH):
            start = HALO - CONV_PAD + r + t
            acc = acc + u_ref[start:start + rows, :] * w[t:t + 1, :]
        y = _layer_norm(acc + cb_ref[...], g_ref[...], b_ref[...])
        o_ref[0, r:r + rows, :] = (y * jax.nn.sigmoid(y)).astype(BF16)


def _conformer_conv(c_in, conv_w, conv_b, ln_g, ln_b, ts=256, rows=64):
    bsz, s, two_ch = c_in.shape
    ch = two_ch // 2
    hb = ts // HALO
    nh = s // HALO
    kern = functools.partial(_conv_kernel, ts=ts, ch=ch, rows=rows)
    vec = lambda: pl.BlockSpec((1, ch), lambda b, i: (0, 0))
    return pl.pallas_call(
        kern,
        grid=(bsz, s // ts),
        in_specs=[pl.BlockSpec((1, HALO, two_ch), lambda b, i: (b, jnp.maximum(i * hb - 1, 0), 0)),
                  pl.BlockSpec((1, ts, two_ch), lambda b, i: (b, i, 0)),
                  pl.BlockSpec((1, HALO, two_ch), lambda b, i: (b, jnp.minimum((i + 1) * hb, nh - 1), 0)),
                  pl.BlockSpec((CONV_WIDTH, ch), lambda b, i: (0, 0)),
                  vec(), vec(), vec()],
        out_specs=pl.BlockSpec((1, ts, ch), lambda b, i: (b, i, 0)),
        out_shape=jax.ShapeDtypeStruct((bsz, s, ch), BF16),
        scratch_shapes=[pltpu.VMEM((ts + 2 * HALO, ch), F32)],
        compiler_params=_cparams("parallel", "parallel"),
        name="conformer_conv",
    )(c_in, c_in, c_in, conv_w, conv_b.reshape(1, ch), ln_g.reshape(1, ch), ln_b.reshape(1, ch))


def _memkv_kernel(mem_ref, wkt_ref, wv_ref, kbd_ref, vbd_ref, *, heads):
    mb = mem_ref[0].astype(BF16)
    kmt = lax.dot_general(wkt_ref[...], mb, _NT, preferred_element_type=F32)
    vm = jnp.dot(mb, wv_ref[...], preferred_element_type=F32)
    w, m = kmt.shape
    dh = w // heads
    row_head = lax.broadcasted_iota(jnp.int32, (w, m), 0) // dh
    col_head = lax.broadcasted_iota(jnp.int32, (m, w), 1) // dh
    for h in range(heads):
        kbd_ref[0, :, h * m:(h + 1) * m] = jnp.where(row_head == h, kmt, 0.0).astype(BF16)
        vbd_ref[0, h * m:(h + 1) * m, :] = jnp.where(col_head == h, vm, 0.0).astype(BF16)


def _memkv(mem, w_mem_kv, mem_w):
    bsz, m, d = mem.shape
    wb = w_mem_kv.astype(BF16)
    wkt, wv = wb[:, :mem_w].T, wb[:, mem_w:]
    kern = functools.partial(_memkv_kernel, heads=MEM_HEADS)
    return pl.pallas_call(
        kern,
        grid=(bsz,),
        in_specs=[pl.BlockSpec((1, m, d), lambda b: (b, 0, 0)),
                  pl.BlockSpec((mem_w, d), lambda b: (0, 0)),
                  pl.BlockSpec((d, mem_w), lambda b: (0, 0))],
        out_specs=[pl.BlockSpec((1, mem_w, MEM_HEADS * m), lambda b: (b, 0, 0)),
                   pl.BlockSpec((1, MEM_HEADS * m, mem_w), lambda b: (b, 0, 0))],
        out_shape=[jax.ShapeDtypeStruct((bsz, mem_w, MEM_HEADS * m), BF16),
                   jax.ShapeDtypeStruct((bsz, MEM_HEADS * m, mem_w), BF16)],
        compiler_params=_cparams("parallel"),
        name="mem_kv",
    )(mem, wkt, wv)


def _memattn_kernel(qm_ref, kbd_ref, vbd_ref, o_ref, *, heads):
    s = jnp.dot(qm_ref[0], kbd_ref[0], preferred_element_type=F32)
    m = s.shape[1] // heads
    w = o_ref.shape[2]
    dh = w // heads
    lane_head = lax.broadcasted_iota(jnp.int32, (1, w), 1) // dh
    ps = []
    scale = jnp.zeros((s.shape[0], w), F32)
    for h in range(heads):
        sh = s[:, h * m:(h + 1) * m]
        p = jnp.exp(sh - jnp.max(sh, axis=-1, keepdims=True))
        scale = jnp.where(lane_head == h, 1.0 / jnp.sum(p, axis=-1, keepdims=True), scale)
        ps.append(p.astype(BF16))
    o = jnp.dot(jnp.concatenate(ps, axis=1), vbd_ref[0], preferred_element_type=F32)
    o_ref[0] = (o * scale).astype(BF16)


def _memattn(qm, kbd, vbd, tq=512):
    bsz, s, w = qm.shape
    hm = kbd.shape[2]
    kern = functools.partial(_memattn_kernel, heads=MEM_HEADS)
    return pl.pallas_call(
        kern,
        grid=(bsz, s // tq),
        in_specs=[pl.BlockSpec((1, tq, w), lambda b, i: (b, i, 0)),
                  pl.BlockSpec((1, w, hm), lambda b, i: (b, 0, 0)),
                  pl.BlockSpec((1, hm, w), lambda b, i: (b, 0, 0))],
        out_specs=pl.BlockSpec((1, tq, w), lambda b, i: (b, i, 0)),
        out_shape=jax.ShapeDtypeStruct((bsz, s, w), BF16),
        compiler_params=_cparams("parallel", "parallel"),
        name="mem_attn",
    )(qm, kbd, vbd)


def _t5_bucket(rel):
    half = N_BUCKETS // 2
    max_exact = half // 2
    ret = (rel > 0).astype(jnp.int32) * half
    n = jnp.abs(rel)
    nf = jnp.maximum(n, 1).astype(jnp.float32)
    large = max_exact + (jnp.log(nf / max_exact) / math.log(MAX_DISTANCE / max_exact)
                         * (half - max_exact)).astype(jnp.int32)
    large = jnp.minimum(large, half - 1)
    return ret + jnp.where(n < max_exact, n, large)


def _bias_tile_kernel(relb_ref, idx_ref, o_ref):
    h = pl.program_id(0)
    idx = idx_ref[0]
    acc = jnp.zeros(idx.shape, F32)
    for bkt in range(N_BUCKETS):
        acc = jnp.where(idx == bkt, relb_ref[bkt, h], acc)
    o_ref[0, 0] = acc * LOG2E


def _bias_tiles(rel_bias, s):
    t = ATT_TILE
    sat = (N_BUCKETS // 4) * (MAX_DISTANCE / (N_BUCKETS // 4)) ** ((N_BUCKETS // 2 - 1 - N_BUCKETS // 4) / (N_BUCKETS // 4))
    assert t + 1 > sat + 1, "attention tile too small for the saturated-bias shortcut"
    kk = jnp.arange(t, dtype=jnp.int32)[:, None]
    qq = jnp.arange(t, dtype=jnp.int32)[None, :]
    idx = jnp.stack([_t5_bucket((d - 1) * t + kk - qq) for d in range(3)])
    far = _t5_bucket(jnp.array([-(s - 1), s - 1], jnp.int32))
    tiles = pl.pallas_call(
        _bias_tile_kernel,
        grid=(DIFF_HEADS, 3),
        in_specs=[pl.BlockSpec(memory_space=pltpu.SMEM),
                  pl.BlockSpec((1, t, t), lambda h, d: (d, 0, 0))],
        out_specs=pl.BlockSpec((1, 1, t, t), lambda h, d: (h, d, 0, 0)),
        out_shape=jax.ShapeDtypeStruct((DIFF_HEADS, 3, t, t), F32),
        compiler_params=_cparams("parallel", "parallel"),
        name="bias_tiles",
    )(rel_bias, idx)
    return tiles, far


def _diffattn_kernel(far_ref, relb_ref, lami_ref, qt_ref, k_ref, vt_ref, bias_ref, g_ref,
                     lq1_ref, lk1_ref, lq2_ref, lk2_ref, o_ref, vaug_ref,
                     s1_ref, s2_ref, p1_ref, p2_ref, *, dh, dv, nk):
    t = ATT_TILE
    h = pl.program_id(1)
    qi = pl.program_id(2)

    @pl.when(qi == 0)
    def _():
        vaug_ref[0:dv, :] = vt_ref[0]
        vaug_ref[dv:, :] = jnp.ones((BF16_ROWS, vaug_ref.shape[1]), BF16)

    c_left = relb_ref[far_ref[0], h] * LOG2E
    c_right = relb_ref[far_ref[1], h] * LOG2E
    qt = qt_ref[0]
    row = lax.broadcasted_iota(jnp.int32, qt.shape, 0)
    qtm = (jnp.where(row < dh, qt, jnp.zeros_like(qt)), jnp.where(row >= dh, qt, jnp.zeros_like(qt)))
    s_refs = (s1_ref, s2_ref)
    p_refs = (p1_ref, p2_ref)

    def colmax(s):
        return jnp.max(s.reshape(t // SUBLANES, SUBLANES, t), axis=0)

    m8 = [jnp.full((SUBLANES, t), -jnp.inf, F32) for _ in range(2)]
    for j in range(nk):
        c_far = jnp.where(j < qi - 1, c_left, jnp.where(j > qi + 1, c_right, -jnp.inf))
        for mp in range(2):
            s = jnp.dot(k_ref[0, j * t:(j + 1) * t, :], qtm[mp], preferred_element_type=F32)
            s_refs[mp][j * t:(j + 1) * t, :] = s
            m8[mp] = jnp.maximum(m8[mp], colmax(s) + c_far)
    for d in range(3):
        j = qi - 1 + d
        valid = jnp.logical_and(j >= 0, j < nk)
        rows = pl.ds(pl.multiple_of(jnp.clip(j, 0, nk - 1) * t, t), t)
        bias = jnp.where(valid, bias_ref[0, d], 0.0)
        for mp in range(2):
            s = s_refs[mp][rows, :] + bias
            s_refs[mp][rows, :] = s
            m8[mp] = jnp.maximum(m8[mp], jnp.where(valid, colmax(s), -jnp.inf))

    outs = []
    for mp in range(2):
        m = jnp.max(m8[mp], axis=0, keepdims=True)
        for j in range(nk):
            c = jnp.where(j < qi - 1, c_left, jnp.where(j > qi + 1, c_right, 0.0))
            p = jnp.exp2(s_refs[mp][j * t:(j + 1) * t, :] - (m - c))
            p_refs[mp][j * t:(j + 1) * t, :] = p.astype(BF16)
        acc = jnp.dot(vaug_ref[...], p_refs[mp][...], preferred_element_type=F32)
        outs.append(acc[0:dv] / acc[dv:dv + 1])
    o1, o2 = outs

    lam_init = lami_ref[0]
    e1 = jnp.exp(jnp.sum(lq1_ref[...] * lk1_ref[...], axis=-1, keepdims=True))
    e2 = jnp.exp(jnp.sum(lq2_ref[...] * lk2_ref[...], axis=-1, keepdims=True))
    lam = e1 - e2 + lam_init
    ot = o1 - lam * o2
    ms = jnp.mean(ot * ot, axis=0, keepdims=True)
    ot = ot * lax.rsqrt(ms + LN_EPS) * g_ref[...] * (1.0 - lam_init)
    o_ref[0] = ot.T.astype(BF16)


def _diffattn(qt, k, vt, bias_tiles, far, rel_bias, lam_init, norm_g, lq1, lk1, lq2, lk2):
    bsz, s, qk_w = k.shape
    diff_w = vt.shape[1]
    dv = diff_w // DIFF_HEADS
    dh = qk_w // (2 * DIFF_HEADS)
    t = ATT_TILE
    nk = s // t
    kern = functools.partial(_diffattn_kernel, dh=dh, dv=dv, nk=nk)
    smem = pl.BlockSpec(memory_space=pltpu.SMEM)
    lvec = lambda: pl.BlockSpec((1, dh), lambda b, h, i: (0, 0))
    return pl.pallas_call(
        kern,
        grid=(bsz, DIFF_HEADS, nk),
        in_specs=[smem, smem, smem,
                  pl.BlockSpec((1, 2 * dh, t), lambda b, h, i: (b, h, i)),
                  pl.BlockSpec((1, s, 2 * dh), lambda b, h, i: (b, 0, h)),
                  pl.BlockSpec((1, dv, s), lambda b, h, i: (b, h, 0)),
                  pl.BlockSpec((1, 3, t, t), lambda b, h, i: (h, 0, 0, 0)),
                  pl.BlockSpec((dv, 1), lambda b, h, i: (0, 0)),
                  lvec(), lvec(), lvec(), lvec()],
        out_specs=pl.BlockSpec((1, t, dv), lambda b, h, i: (b, i, h)),
        out_shape=jax.ShapeDtypeStruct((bsz, s, diff_w), BF16),
        scratch_shapes=[pltpu.VMEM((dv + BF16_ROWS, s), BF16),
                        pltpu.VMEM((s, t), F32), pltpu.VMEM((s, t), F32),
                        pltpu.VMEM((s, t), BF16), pltpu.VMEM((s, t), BF16)],
        compiler_params=_cparams("parallel", "parallel", "arbitrary"),
        name="diff_attn",
    )(far, rel_bias, jnp.full((1,), lam_init, F32), qt, k, vt, bias_tiles, norm_g.reshape(dv, 1),
      lq1.reshape(1, dh), lk1.reshape(1, dh), lq2.reshape(1, dh), lk2.reshape(1, dh))


def _outproj_kernel(x_ref, c_ref, d_ref, m_ref, wc_ref, wd_ref, wm_ref, b_ref, g_ref, beta_ref, o_ref):
    y = jnp.dot(c_ref[...], wc_ref[...], preferred_element_type=F32)
    y = y + jnp.dot(d_ref[...], wd_ref[...], preferred_element_type=F32)
    y = y + jnp.dot(m_ref[...], wm_ref[...], preferred_element_type=F32)
    z = ALPHA * x_ref[...] + (y + b_ref[...])
    o_ref[...] = _layer_norm(z, g_ref[...], beta_ref[...])


def _outproj_ln(x2d, conv_out, diff_out, mem_out, w_out, b_out, g, beta, tm=512):
    m, d = x2d.shape
    cw, dw, mw = conv_out.shape[1], diff_out.shape[1], mem_out.shape[1]
    wb = w_out.astype(BF16)
    rows = lambda w: pl.BlockSpec((tm, w), lambda i: (i, 0))
    const = lambda shape: pl.BlockSpec(shape, lambda i: (0, 0))
    return pl.pallas_call(
        _outproj_kernel,
        grid=(m // tm,),
        in_specs=[rows(d), rows(cw), rows(dw), rows(mw),
                  const((cw, d)), const((dw, d)), const((mw, d)),
                  const((1, d)), const((1, d)), const((1, d))],
        out_specs=rows(d),
        out_shape=jax.ShapeDtypeStruct((m, d), F32),
        compiler_params=_cparams("parallel"),
        name="outproj_ln",
    )(x2d, conv_out, diff_out, mem_out, wb[:cw], wb[cw:cw + dw], wb[cw + dw:],
      b_out.reshape(1, d), g.reshape(1, d), beta.reshape(1, d))


def _mlp_kernel(x_ref, wu_ref, wd_ref, g_ref, beta_ref, o_ref, *, chunk):
    x = x_ref[...]
    xb = x.astype(BF16)
    ff = jnp.zeros(x.shape, F32)
    for c in range(0, wu_ref.shape[1], chunk):
        hcol = jnp.maximum(jnp.dot(xb, wu_ref[:, c:c + chunk], preferred_element_type=F32), 0.0)
        ff = ff + jnp.dot((hcol * hcol).astype(BF16), wd_ref[c:c + chunk, :], preferred_element_type=F32)
    o_ref[...] = _layer_norm(ALPHA * x + ff, g_ref[...], beta_ref[...])


def _mlp_ln(x2d, w_up, w_down, g, beta, tm=512, chunk=1024):
    m, d = x2d.shape
    dff = w_up.shape[1]
    kern = functools.partial(_mlp_kernel, chunk=chunk)
    const = lambda shape: pl.BlockSpec(shape, lambda i: (0, 0))
    return pl.pallas_call(
        kern,
        grid=(m // tm,),
        in_specs=[pl.BlockSpec((tm, d), lambda i: (i, 0)),
                  const((d, dff)), const((dff, d)), const((1, d)), const((1, d))],
        out_specs=pl.BlockSpec((tm, d), lambda i: (i, 0)),
        out_shape=jax.ShapeDtypeStruct((m, d), F32),
        compiler_params=_cparams("parallel"),
        name="mlp_ln",
    )(x2d, w_up.astype(BF16), w_down.astype(BF16), g.reshape(1, d), beta.reshape(1, d))


def kernel(x, mem, emb_ln_g, emb_ln_b, rel_bias, w_in, b_in, conv_w, conv_b, conv_ln_g, conv_ln_b,
           lambda_q1, lambda_k1, lambda_q2, lambda_k2, diff_norm_g, w_mem_kv, w_out, b_out,
           ln1_g, ln1_b, w_up, w_down, ln2_g, ln2_b):
    bsz, s, d = x.shape
    conv_ch = conv_w.shape[2]
    dh = lambda_q1.shape[1]
    qk_w = DIFF_HEADS * 2 * dh
    diff_w = DIFF_HEADS * diff_norm_g.shape[1]
    mem_w = w_mem_kv.shape[2] // 2
    dims = (conv_ch, qk_w, diff_w, mem_w)
    assert s % ATT_TILE == 0 and 2 * dh == LANES and diff_norm_g.shape[1] == LANES

    xs = _embed_ln(x.reshape(bsz * s, d), emb_ln_g, emb_ln_b)
    bias_tiles, far = _bias_tiles(rel_bias, s)
    for l in range(DEPTH):
        lam_init = 0.8 - 0.6 * math.exp(-0.3 * l)
        c_in, k, qt, vt, qm = _inproj(xs.reshape(bsz, s, d), w_in[l], b_in[l], dims)
        conv_out = _conformer_conv(c_in, conv_w[l], conv_b[l], conv_ln_g[l], conv_ln_b[l])
        diff_out = _diffattn(qt, k, vt, bias_tiles, far, rel_bias, lam_init, diff_norm_g[l],
                             lambda_q1[l], lambda_k1[l], lambda_q2[l], lambda_k2[l])
        kbd, vbd = _memkv(mem, w_mem_kv[l], mem_w)
        mem_out = _memattn(qm, kbd, vbd)
        xs = _outproj_ln(xs, conv_out.reshape(bsz * s, -1), diff_out.reshape(bsz * s, -1),
                         mem_out.reshape(bsz * s, -1), w_out[l], b_out[l], ln1_g[l], ln1_b[l])
        xs = _mlp_ln(xs, w_up[l], w_down[l], ln2_g[l], ln2_b[l])
    return xs.reshape(bsz, s, d)
```

```python
import functools
import math

import jax
import jax.numpy as jnp
from jax import lax
from jax.experimental import pallas as pl
from jax.experimental.pallas import tpu as pltpu

F32 = jnp.float32
BF16 = jnp.bfloat16

DEPTH = 4
CONV_WIDTH = 31
CONV_PAD = (CONV_WIDTH - 1) // 2
DIFF_HEADS = 4
MEM_HEADS = 4
N_BUCKETS = 32
MAX_DISTANCE = 128
ALPHA = (2.0 * DEPTH) ** 0.25
LN_EPS = 1e-5
LOG2E = math.log2(math.e)

LANES = 128
SUBLANES = 8
BF16_ROWS = 16
MXU_DIM = 256
VMEM_LIMIT = 56 * 1024 * 1024

ATT_TILE = MXU_DIM
QT_PER_STEP = 2
PV_CHUNK = 4
HALO = 16

_NT = (((1,), (1,)), ((), ()))


def _cparams(*sem):
    return pltpu.CompilerParams(dimension_semantics=sem, vmem_limit_bytes=VMEM_LIMIT)


def _layer_norm(z, g, b):
    mu = jnp.mean(z, axis=-1, keepdims=True)
    zc = z - mu
    var = jnp.mean(zc * zc, axis=-1, keepdims=True)
    return zc * lax.rsqrt(var + LN_EPS) * g + b


def _ln_kernel(x_ref, g_ref, b_ref, o_ref):
    o_ref[...] = _layer_norm(x_ref[...], g_ref[...], b_ref[...])


def _embed_ln(x2d, g, b, tm=1024):
    m, d = x2d.shape
    return pl.pallas_call(
        _ln_kernel,
        grid=(m // tm,),
        in_specs=[pl.BlockSpec((tm, d), lambda i: (i, 0)),
                  pl.BlockSpec((1, d), lambda i: (0, 0)),
                  pl.BlockSpec((1, d), lambda i: (0, 0))],
        out_specs=pl.BlockSpec((tm, d), lambda i: (i, 0)),
        out_shape=jax.ShapeDtypeStruct((m, d), F32),
        compiler_params=_cparams("parallel"),
        name="embed_ln",
    )(x2d, g.reshape(1, d), b.reshape(1, d))


def _inproj_kernel(x_ref, wc_ref, wk_ref, wqt_ref, wvt_ref, wm_ref,
                   bc_ref, bk_ref, bqt_ref, bvt_ref, bm_ref,
                   c_ref, k_ref, qt_ref, vt_ref, qm_ref, *, q_scale, qm_scale):
    xb = x_ref[0].astype(BF16)
    c_ref[0] = jnp.dot(xb, wc_ref[...], preferred_element_type=F32) + bc_ref[...]
    k_ref[0] = (jnp.dot(xb, wk_ref[...], preferred_element_type=F32) + bk_ref[...]).astype(BF16)
    qt = lax.dot_general(wqt_ref[...], xb, _NT, preferred_element_type=F32) + bqt_ref[...]
    qt_ref[0] = (qt * q_scale).astype(BF16)
    vt = lax.dot_general(wvt_ref[...], xb, _NT, preferred_element_type=F32) + bvt_ref[...]
    vt_ref[0] = vt.astype(BF16)
    qm = jnp.dot(xb, wm_ref[...], preferred_element_type=F32) + bm_ref[...]
    qm_ref[0] = (qm * qm_scale).astype(BF16)


def _inproj(x, w_in, b_in, dims, tm=512):
    bsz, s, d = x.shape
    conv_ch, qk_w, diff_w, mem_w = dims
    s0 = 2 * conv_ch
    s1 = s0 + qk_w
    s2 = s1 + qk_w
    s3 = s2 + diff_w
    wb = w_in.astype(BF16)
    wc, wq, wk, wv, wm = wb[:, :s0], wb[:, s0:s1], wb[:, s1:s2], wb[:, s2:s3], wb[:, s3:]
    bc, bq, bk, bv, bm = b_in[:s0], b_in[s0:s1], b_in[s1:s2], b_in[s2:s3], b_in[s3:]
    dh = qk_w // (2 * DIFF_HEADS)
    mem_dh = mem_w // MEM_HEADS
    kern = functools.partial(_inproj_kernel, q_scale=dh ** -0.5 * LOG2E, qm_scale=mem_dh ** -0.5)
    const = lambda shape: pl.BlockSpec(shape, lambda b, i: (0, 0))
    return pl.pallas_call(
        kern,
        grid=(bsz, s // tm),
        in_specs=[pl.BlockSpec((1, tm, d), lambda b, i: (b, i, 0)),
                  const((d, s0)), const((d, qk_w)), const((qk_w, d)), const((diff_w, d)), const((d, mem_w)),
                  const((1, s0)), const((1, qk_w)), const((qk_w, 1)), const((diff_w, 1)), const((1, mem_w))],
        out_specs=[pl.BlockSpec((1, tm, s0), lambda b, i: (b, i, 0)),
                   pl.BlockSpec((1, tm, qk_w), lambda b, i: (b, i, 0)),
                   pl.BlockSpec((1, qk_w, tm), lambda b, i: (b, 0, i)),
                   pl.BlockSpec((1, diff_w, tm), lambda b, i: (b, 0, i)),
                   pl.BlockSpec((1, tm, mem_w), lambda b, i: (b, i, 0))],
        out_shape=[jax.ShapeDtypeStruct((bsz, s, s0), F32),
                   jax.ShapeDtypeStruct((bsz, s, qk_w), BF16),
                   jax.ShapeDtypeStruct((bsz, qk_w, s), BF16),
                   jax.ShapeDtypeStruct((bsz, diff_w, s), BF16),
                   jax.ShapeDtypeStruct((bsz, s, mem_w), BF16)],
        compiler_params=_cparams("parallel", "parallel"),
        name="inproj",
    )(x, wc, wk, wq.T, wv.T, wm,
      bc.reshape(1, -1), bk.reshape(1, -1), bq.reshape(-1, 1), bv.reshape(-1, 1), bm.reshape(1, -1))


def _conv_kernel(prev_ref, cur_ref, next_ref, w_ref, cb_ref, g_ref, b_ref, o_ref, u_ref, us_ref, *, ts, ch, rows):
    i = pl.program_id(1)
    n = pl.num_programs(1)

    def glu(c):
        return c[:, :ch] * jax.nn.sigmoid(c[:, ch:])

    u_ref[0:HALO, :] = jnp.where(i > 0, glu(prev_ref[0]), 0.0)
    u_ref[HALO:HALO + ts, :] = glu(cur_ref[0])
    u_ref[HALO + ts:, :] = jnp.where(i < n - 1, glu(next_ref[0]), 0.0)

    for ph in range(SUBLANES):
        us_ref[ph] = u_ref[ph:ph + us_ref.shape[1], :]

    w = w_ref[...]
    for r in range(0, ts, rows):
        acc = jnp.zeros((rows, ch), F32)
        for t in range(CONV_WIDTH):
            tiles, ph = divmod(HALO - CONV_PAD + t, SUBLANES)
            start = r + tiles * SUBLANES
            acc = acc + us_ref[ph, start:start + rows, :] * w[t:t + 1, :]
        y = _layer_norm(acc + cb_ref[...], g_ref[...], b_ref[...])
        o_ref[0, r:r + rows, :] = (y * jax.nn.sigmoid(y)).astype(BF16)


def _conformer_conv(c_in, conv_w, conv_b, ln_g, ln_b, ts=256, rows=64):
    bsz, s, two_ch = c_in.shape
    ch = two_ch // 2
    hb = ts // HALO
    nh = s // HALO
    kern = functools.partial(_conv_kernel, ts=ts, ch=ch, rows=rows)
    vec = lambda: pl.BlockSpec((1, ch), lambda b, i: (0, 0))
    return pl.pallas_call(
        kern,
        grid=(bsz, s // ts),
        in_specs=[pl.BlockSpec((1, HALO, two_ch), lambda b, i: (b, jnp.maximum(i * hb - 1, 0), 0)),
                  pl.BlockSpec((1, ts, two_ch), lambda b, i: (b, i, 0)),
                  pl.BlockSpec((1, HALO, two_ch), lambda b, i: (b, jnp.minimum((i + 1) * hb, nh - 1), 0)),
                  pl.BlockSpec((CONV_WIDTH, ch), lambda b, i: (0, 0)),
                  vec(), vec(), vec()],
        out_specs=pl.BlockSpec((1, ts, ch), lambda b, i: (b, i, 0)),
        out_shape=jax.ShapeDtypeStruct((bsz, s, ch), BF16),
        scratch_shapes=[pltpu.VMEM((ts + 2 * HALO, ch), F32),
                        pltpu.VMEM((SUBLANES, ts + 2 * HALO - SUBLANES, ch), F32)],
        compiler_params=_cparams("parallel", "parallel"),
        name="conformer_conv",
    )(c_in, c_in, c_in, conv_w, conv_b.reshape(1, ch), ln_g.reshape(1, ch), ln_b.reshape(1, ch))


def _memkv_kernel(mem_ref, wkt_ref, wv_ref, kbd_ref, vbd_ref, *, heads):
    mb = mem_ref[0].astype(BF16)
    kmt = lax.dot_general(wkt_ref[...], mb, _NT, preferred_element_type=F32)
    vm = jnp.dot(mb, wv_ref[...], preferred_element_type=F32)
    w, m = kmt.shape
    dh = w // heads
    row_head = lax.broadcasted_iota(jnp.int32, (w, m), 0) // dh
    col_head = lax.broadcasted_iota(jnp.int32, (m, w), 1) // dh
    for h in range(heads):
        kbd_ref[0, :, h * m:(h + 1) * m] = jnp.where(row_head == h, kmt, 0.0).astype(BF16)
        vbd_ref[0, h * m:(h + 1) * m, :] = jnp.where(col_head == h, vm, 0.0).astype(BF16)


def _memkv(mem, w_mem_kv, mem_w):
    bsz, m, d = mem.shape
    wb = w_mem_kv.astype(BF16)
    wkt, wv = wb[:, :mem_w].T, wb[:, mem_w:]
    kern = functools.partial(_memkv_kernel, heads=MEM_HEADS)
    return pl.pallas_call(
        kern,
        grid=(bsz,),
        in_specs=[pl.BlockSpec((1, m, d), lambda b: (b, 0, 0)),
                  pl.BlockSpec((mem_w, d), lambda b: (0, 0)),
                  pl.BlockSpec((d, mem_w), lambda b: (0, 0))],
        out_specs=[pl.BlockSpec((1, mem_w, MEM_HEADS * m), lambda b: (b, 0, 0)),
                   pl.BlockSpec((1, MEM_HEADS * m, mem_w), lambda b: (b, 0, 0))],
        out_shape=[jax.ShapeDtypeStruct((bsz, mem_w, MEM_HEADS * m), BF16),
                   jax.ShapeDtypeStruct((bsz, MEM_HEADS * m, mem_w), BF16)],
        compiler_params=_cparams("parallel"),
        name="mem_kv",
    )(mem, wkt, wv)


def _memattn_kernel(qm_ref, kbd_ref, vbd_ref, o_ref, *, heads):
    s = jnp.dot(qm_ref[0], kbd_ref[0], preferred_element_type=F32)
    m = s.shape[1] // heads
    w = o_ref.shape[2]
    dh = w // heads
    lane_head = lax.broadcasted_iota(jnp.int32, (1, w), 1) // dh
    ps = []
    scale = jnp.zeros((s.shape[0], w), F32)
    for h in range(heads):
        sh = s[:, h * m:(h + 1) * m]
        p = jnp.exp(sh - jnp.max(sh, axis=-1, keepdims=True))
        scale = jnp.where(lane_head == h, 1.0 / jnp.sum(p, axis=-1, keepdims=True), scale)
        ps.append(p.astype(BF16))
    o = jnp.dot(jnp.concatenate(ps, axis=1), vbd_ref[0], preferred_element_type=F32)
    o_ref[0] = (o * scale).astype(BF16)


def _memattn(qm, kbd, vbd, tq=512):
    bsz, s, w = qm.shape
    hm = kbd.shape[2]
    kern = functools.partial(_memattn_kernel, heads=MEM_HEADS)
    return pl.pallas_call(
        kern,
        grid=(bsz, s // tq),
        in_specs=[pl.BlockSpec((1, tq, w), lambda b, i: (b, i, 0)),
                  pl.BlockSpec((1, w, hm), lambda b, i: (b, 0, 0)),
                  pl.BlockSpec((1, hm, w), lambda b, i: (b, 0, 0))],
        out_specs=pl.BlockSpec((1, tq, w), lambda b, i: (b, i, 0)),
        out_shape=jax.ShapeDtypeStruct((bsz, s, w), BF16),
        compiler_params=_cparams("parallel", "parallel"),
        name="mem_attn",
    )(qm, kbd, vbd)


def _t5_bucket(rel):
    half = N_BUCKETS // 2
    max_exact = half // 2
    ret = (rel > 0).astype(jnp.int32) * half
    n = jnp.abs(rel)
    nf = jnp.maximum(n, 1).astype(jnp.float32)
    large = max_exact + (jnp.log(nf / max_exact) / math.log(MAX_DISTANCE / max_exact)
                         * (half - max_exact)).astype(jnp.int32)
    large = jnp.minimum(large, half - 1)
    return ret + jnp.where(n < max_exact, n, large)


def _bias_tile_kernel(relb_ref, idx_ref, o_ref):
    h = pl.program_id(0)
    idx = idx_ref[0]
    acc = jnp.zeros(idx.shape, F32)
    for bkt in range(N_BUCKETS):
        acc = jnp.where(idx == bkt, relb_ref[bkt, h], acc)
    o_ref[0, 0] = acc * LOG2E


def _bias_tiles(rel_bias, s):
    t = ATT_TILE
    sat = (N_BUCKETS // 4) * (MAX_DISTANCE / (N_BUCKETS // 4)) ** ((N_BUCKETS // 2 - 1 - N_BUCKETS // 4) / (N_BUCKETS // 4))
    assert t + 1 > sat + 1, "attention tile too small for the saturated-bias shortcut"
    kk = jnp.arange(t, dtype=jnp.int32)[:, None]
    qq = jnp.arange(t, dtype=jnp.int32)[None, :]
    idx = jnp.stack([_t5_bucket((d - 1) * t + kk - qq) for d in range(3)])
    far = _t5_bucket(jnp.array([-(s - 1), s - 1], jnp.int32))
    tiles = pl.pallas_call(
        _bias_tile_kernel,
        grid=(DIFF_HEADS, 3),
        in_specs=[pl.BlockSpec(memory_space=pltpu.SMEM),
                  pl.BlockSpec((1, t, t), lambda h, d: (d, 0, 0))],
        out_specs=pl.BlockSpec((1, 1, t, t), lambda h, d: (h, d, 0, 0)),
        out_shape=jax.ShapeDtypeStruct((DIFF_HEADS, 3, t, t), F32),
        compiler_params=_cparams("parallel", "parallel"),
        name="bias_tiles",
    )(rel_bias, idx)
    return tiles, far


def _diffattn_kernel(far_ref, relb_ref, lami_ref, qt_ref, k_ref, vt_ref, bias_ref, g_ref,
                     lq1_ref, lk1_ref, lq2_ref, lk2_ref, o_ref, vaug_ref, s_ref, p_ref, *, dh, dv, nk):
    t = ATT_TILE
    h = pl.program_id(1)
    pair = pl.program_id(2)

    @pl.when(pair == 0)
    def _():
        vaug_ref[0:dv, :] = vt_ref[0]
        vaug_ref[dv:, :] = jnp.ones((BF16_ROWS, vaug_ref.shape[1]), BF16)

    c_left = relb_ref[far_ref[0], h] * LOG2E
    c_right = relb_ref[far_ref[1], h] * LOG2E
    row = lax.broadcasted_iota(jnp.int32, (2 * dh, t), 0)
    qis, qtms = [], []
    for u in range(QT_PER_STEP):
        qt = qt_ref[0, :, u * t:(u + 1) * t]
        qis.append(pair * QT_PER_STEP + u)
        qtms.append((jnp.where(row < dh, qt, jnp.zeros_like(qt)), jnp.where(row >= dh, qt, jnp.zeros_like(qt))))
    m8 = [[jnp.full((SUBLANES, t), -jnp.inf, F32) for _ in range(2)] for _ in range(QT_PER_STEP)]
    mrow = [[None, None] for _ in range(QT_PER_STEP)]

    def colmax(s):
        return jnp.max(s.reshape(t // SUBLANES, SUBLANES, t), axis=0)

    def far_bias(u, j, near):
        return jnp.where(j < qis[u] - 1, c_left, jnp.where(j > qis[u] + 1, c_right, near))

    def score_tile(u, j):
        c_far = far_bias(u, j, -jnp.inf)
        for mp in range(2):
            s = jnp.dot(k_ref[0, j * t:(j + 1) * t, :], qtms[u][mp], preferred_element_type=F32)
            s_ref[u, mp, j * t:(j + 1) * t, :] = s
            m8[u][mp] = jnp.maximum(m8[u][mp], colmax(s) + c_far)

    def finish_scores(u):
        for d in range(3):
            j = qis[u] - 1 + d
            valid = jnp.logical_and(j >= 0, j < nk)
            rows = pl.ds(pl.multiple_of(jnp.clip(j, 0, nk - 1) * t, t), t)
            bias = jnp.where(valid, bias_ref[0, d], 0.0)
            for mp in range(2):
                s = s_ref[u, mp, rows, :] + bias
                s_ref[u, mp, rows, :] = s
                m8[u][mp] = jnp.maximum(m8[u][mp], jnp.where(valid, colmax(s), -jnp.inf))
        for mp in range(2):
            mrow[u][mp] = jnp.max(m8[u][mp], axis=0, keepdims=True)

    def exp_tile(u, j):
        c = far_bias(u, j, 0.0)
        for mp in range(2):
            p = jnp.exp2(s_ref[u, mp, j * t:(j + 1) * t, :] - (mrow[u][mp] - c))
            p_ref[u, mp, j * t:(j + 1) * t, :] = p.astype(BF16)

    def value_chunk(u, c, acc):
        keys = slice(c * PV_CHUNK * t, (c + 1) * PV_CHUNK * t)
        out = []
        for mp in range(2):
            part = jnp.dot(vaug_ref[:, keys], p_ref[u, mp, keys, :], preferred_element_type=F32)
            out.append(part if acc is None else acc[mp] + part)
        return out

    def finish(u, acc):
        o1 = acc[0][0:dv] / acc[0][dv:dv + 1]
        o2 = acc[1][0:dv] / acc[1][dv:dv + 1]
        lam_init = lami_ref[0]
        e1 = jnp.exp(jnp.sum(lq1_ref[...] * lk1_ref[...], axis=-1, keepdims=True))
        e2 = jnp.exp(jnp.sum(lq2_ref[...] * lk2_ref[...], axis=-1, keepdims=True))
        lam = e1 - e2 + lam_init
        ot = o1 - lam * o2
        ms = jnp.mean(ot * ot, axis=0, keepdims=True)
        ot = ot * lax.rsqrt(ms + LN_EPS) * g_ref[...] * (1.0 - lam_init)
        o_ref[0, u * t:(u + 1) * t, :] = ot.T.astype(BF16)

    n_chunks = nk // PV_CHUNK
    for j in range(nk):
        score_tile(0, j)
    finish_scores(0)
    for j in range(nk):
        score_tile(1, j)
        exp_tile(0, j)
    finish_scores(1)
    acc0 = None
    for j in range(nk):
        exp_tile(1, j)
        if j % PV_CHUNK == 0:
            acc0 = value_chunk(0, j // PV_CHUNK, acc0)
    finish(0, acc0)
    acc1 = None
    for c in range(n_chunks):
        acc1 = value_chunk(1, c, acc1)
    finish(1, acc1)


def _diffattn(qt, k, vt, bias_tiles, far, rel_bias, lam_init, norm_g, lq1, lk1, lq2, lk2):
    bsz, s, qk_w = k.shape
    diff_w = vt.shape[1]
    dv = diff_w // DIFF_HEADS
    dh = qk_w // (2 * DIFF_HEADS)
    t = ATT_TILE
    nk = s // t
    tq = QT_PER_STEP * t
    kern = functools.partial(_diffattn_kernel, dh=dh, dv=dv, nk=nk)
    smem = pl.BlockSpec(memory_space=pltpu.SMEM)
    lvec = lambda: pl.BlockSpec((1, dh), lambda b, h, i: (0, 0))
    return pl.pallas_call(
        kern,
        grid=(bsz, DIFF_HEADS, s // tq),
        in_specs=[smem, smem, smem,
                  pl.BlockSpec((1, 2 * dh, tq), lambda b, h, i: (b, h, i)),
                  pl.BlockSpec((1, s, 2 * dh), lambda b, h, i: (b, 0, h)),
                  pl.BlockSpec((1, dv, s), lambda b, h, i: (b, h, 0)),
                  pl.BlockSpec((1, 3, t, t), lambda b, h, i: (h, 0, 0, 0)),
                  pl.BlockSpec((dv, 1), lambda b, h, i: (0, 0)),
                  lvec(), lvec(), lvec(), lvec()],
        out_specs=pl.BlockSpec((1, tq, dv), lambda b, h, i: (b, i, h)),
        out_shape=jax.ShapeDtypeStruct((bsz, s, diff_w), BF16),
        scratch_shapes=[pltpu.VMEM((dv + BF16_ROWS, s), BF16),
                        pltpu.VMEM((QT_PER_STEP, 2, s, t), F32),
                        pltpu.VMEM((QT_PER_STEP, 2, s, t), BF16)],
        compiler_params=_cparams("parallel", "parallel", "arbitrary"),
        name="diff_attn",
    )(far, rel_bias, jnp.full((1,), lam_init, F32), qt, k, vt, bias_tiles, norm_g.reshape(dv, 1),
      lq1.reshape(1, dh), lk1.reshape(1, dh), lq2.reshape(1, dh), lk2.reshape(1, dh))


def _outproj_kernel(x_ref, c_ref, d_ref, m_ref, w_ref, b_ref, g_ref, beta_ref, o_ref, *, rows):
    for r in range(0, x_ref.shape[0], rows):
        sl = slice(r, r + rows)
        mixed = jnp.concatenate([c_ref[sl, :], d_ref[sl, :], m_ref[sl, :]], axis=1)
        y = jnp.dot(mixed, w_ref[...], preferred_element_type=F32)
        z = ALPHA * x_ref[sl, :] + (y + b_ref[...])
        o_ref[sl, :] = _layer_norm(z, g_ref[...], beta_ref[...])


def _outproj_ln(x2d, conv_out, diff_out, mem_out, w_out, b_out, g, beta, tm=512, rows=128):
    m, d = x2d.shape
    cw, dw, mw = conv_out.shape[1], diff_out.shape[1], mem_out.shape[1]
    kern = functools.partial(_outproj_kernel, rows=rows)
    rowblk = lambda w: pl.BlockSpec((tm, w), lambda i: (i, 0))
    const = lambda shape: pl.BlockSpec(shape, lambda i: (0, 0))
    return pl.pallas_call(
        kern,
        grid=(m // tm,),
        in_specs=[rowblk(d), rowblk(cw), rowblk(dw), rowblk(mw),
                  const((cw + dw + mw, d)),
                  const((1, d)), const((1, d)), const((1, d))],
        out_specs=rowblk(d),
        out_shape=jax.ShapeDtypeStruct((m, d), F32),
        compiler_params=_cparams("parallel"),
        name="outproj_ln",
    )(x2d, conv_out, diff_out, mem_out, w_out.astype(BF16),
      b_out.reshape(1, d), g.reshape(1, d), beta.reshape(1, d))


def _mlp_kernel(x_ref, wu_ref, wd_ref, g_ref, beta_ref, o_ref, *, chunk):
    x = x_ref[...]
    xb = x.astype(BF16)
    ff = jnp.zeros(x.shape, F32)
    for c in range(0, wu_ref.shape[1], chunk):
        hcol = jnp.maximum(jnp.dot(xb, wu_ref[:, c:c + chunk], preferred_element_type=F32), 0.0)
        ff = ff + jnp.dot((hcol * hcol).astype(BF16), wd_ref[c:c + chunk, :], preferred_element_type=F32)
    o_ref[...] = _layer_norm(ALPHA * x + ff, g_ref[...], beta_ref[...])


def _mlp_ln(x2d, w_up, w_down, g, beta, tm=512, chunk=1024):
    m, d = x2d.shape
    dff = w_up.shape[1]
    kern = functools.partial(_mlp_kernel, chunk=chunk)
    const = lambda shape: pl.BlockSpec(shape, lambda i: (0, 0))
    return pl.pallas_call(
        kern,
        grid=(m // tm,),
        in_specs=[pl.BlockSpec((tm, d), lambda i: (i, 0)),
                  const((d, dff)), const((dff, d)), const((1, d)), const((1, d))],
        out_specs=pl.BlockSpec((tm, d), lambda i: (i, 0)),
        out_shape=jax.ShapeDtypeStruct((m, d), F32),
        compiler_params=_cparams("parallel"),
        name="mlp_ln",
    )(x2d, w_up.astype(BF16), w_down.astype(BF16), g.reshape(1, d), beta.reshape(1, d))


def kernel(x, mem, emb_ln_g, emb_ln_b, rel_bias, w_in, b_in, conv_w, conv_b, conv_ln_g, conv_ln_b,
           lambda_q1, lambda_k1, lambda_q2, lambda_k2, diff_norm_g, w_mem_kv, w_out, b_out,
           ln1_g, ln1_b, w_up, w_down, ln2_g, ln2_b):
    bsz, s, d = x.shape
    conv_ch = conv_w.shape[2]
    dh = lambda_q1.shape[1]
    qk_w = DIFF_HEADS * 2 * dh
    diff_w = DIFF_HEADS * diff_norm_g.shape[1]
    mem_w = w_mem_kv.shape[2] // 2
    dims = (conv_ch, qk_w, diff_w, mem_w)
    assert s % (QT_PER_STEP * ATT_TILE) == 0 and (s // ATT_TILE) % PV_CHUNK == 0 and 2 * dh == LANES and diff_norm_g.shape[1] == LANES

    xs = _embed_ln(x.reshape(bsz * s, d), emb_ln_g, emb_ln_b)
    bias_tiles, far = _bias_tiles(rel_bias, s)
    for l in range(DEPTH):
        lam_init = 0.8 - 0.6 * math.exp(-0.3 * l)
        c_in, k, qt, vt, qm = _inproj(xs.reshape(bsz, s, d), w_in[l], b_in[l], dims)
        conv_out = _conformer_conv(c_in, conv_w[l], conv_b[l], conv_ln_g[l], conv_ln_b[l])
        diff_out = _diffattn(qt, k, vt, bias_tiles, far, rel_bias, lam_init, diff_norm_g[l],
                             lambda_q1[l], lambda_k1[l], lambda_q2[l], lambda_k2[l])
        kbd, vbd = _memkv(mem, w_mem_kv[l], mem_w)
        mem_out = _memattn(qm, kbd, vbd)
        xs = _outproj_ln(xs, conv_out.reshape(bsz * s, -1), diff_out.reshape(bsz * s, -1),
                         mem_out.reshape(bsz * s, -1), w_out[l], b_out[l], ln1_g[l], ln1_b[l])
        xs = _mlp_ln(xs, w_up[l], w_down[l], ln2_g[l], ln2_b[l])
    return xs.reshape(bsz, s, d)
```

```python
import functools
import math

import jax
import jax.numpy as jnp
from jax import lax
from jax.experimental import pallas as pl
from jax.experimental.pallas import tpu as pltpu

F32 = jnp.float32
BF16 = jnp.bfloat16

DEPTH = 4
CONV_WIDTH = 31
CONV_PAD = (CONV_WIDTH - 1) // 2
DIFF_HEADS = 4
MEM_HEADS = 4
N_BUCKETS = 32
MAX_DISTANCE = 128
ALPHA = (2.0 * DEPTH) ** 0.25
LN_EPS = 1e-5
LOG2E = math.log2(math.e)

LANES = 128
SUBLANES = 8
BF16_ROWS = 16
MXU_DIM = 256
VMEM_LIMIT = 56 * 1024 * 1024

ATT_TILE = MXU_DIM
QT_PER_STEP = 2
PV_CHUNK = 4
HALO = 16

_NT = (((1,), (1,)), ((), ()))


def _cparams(*sem):
    return pltpu.CompilerParams(dimension_semantics=sem, vmem_limit_bytes=VMEM_LIMIT)


def _layer_norm(z, g, b):
    mu = jnp.mean(z, axis=-1, keepdims=True)
    zc = z - mu
    var = jnp.mean(zc * zc, axis=-1, keepdims=True)
    return zc * lax.rsqrt(var + LN_EPS) * g + b


def _ln_kernel(x_ref, g_ref, b_ref, o_ref):
    o_ref[...] = _layer_norm(x_ref[...], g_ref[...], b_ref[...])


def _embed_ln(x2d, g, b, tm=1024):
    m, d = x2d.shape
    return pl.pallas_call(
        _ln_kernel,
        grid=(m // tm,),
        in_specs=[pl.BlockSpec((tm, d), lambda i: (i, 0)),
                  pl.BlockSpec((1, d), lambda i: (0, 0)),
                  pl.BlockSpec((1, d), lambda i: (0, 0))],
        out_specs=pl.BlockSpec((tm, d), lambda i: (i, 0)),
        out_shape=jax.ShapeDtypeStruct((m, d), F32),
        compiler_params=_cparams("parallel"),
        name="embed_ln",
    )(x2d, g.reshape(1, d), b.reshape(1, d))


def _inproj_conv_kernel(x_ref, xp_ref, xn_ref, wc_ref, wk_ref, wqt_ref, wvt_ref, wm_ref,
                        bc_ref, bk_ref, bqt_ref, bvt_ref, bm_ref, cw_ref, cb_ref, cg_ref, cbeta_ref,
                        conv_ref, k_ref, qt_ref, vt_ref, qm_ref, u_ref, us_ref, *, q_scale, qm_scale, ch, rows):
    i = pl.program_id(1)
    n = pl.num_programs(1)
    tm = x_ref.shape[1]
    xb = x_ref[0].astype(BF16)

    def glu(c):
        return c[:, :ch] * jax.nn.sigmoid(c[:, ch:])

    halo = jnp.concatenate([xp_ref[0], xn_ref[0]], axis=0).astype(BF16)
    c_halo = jnp.dot(halo, wc_ref[...], preferred_element_type=F32) + bc_ref[...]
    c_cur = jnp.dot(xb, wc_ref[...], preferred_element_type=F32) + bc_ref[...]
    u_ref[0:HALO, :] = jnp.where(i > 0, glu(c_halo[:HALO]), 0.0)
    u_ref[HALO:HALO + tm, :] = glu(c_cur)
    u_ref[HALO + tm:, :] = jnp.where(i < n - 1, glu(c_halo[HALO:]), 0.0)
    for ph in range(SUBLANES):
        us_ref[ph] = u_ref[ph:ph + us_ref.shape[1], :]

    k_ref[0] = (jnp.dot(xb, wk_ref[...], preferred_element_type=F32) + bk_ref[...]).astype(BF16)
    qt = lax.dot_general(wqt_ref[...], xb, _NT, preferred_element_type=F32) + bqt_ref[...]
    qt_ref[0] = (qt * q_scale).astype(BF16)
    vt = lax.dot_general(wvt_ref[...], xb, _NT, preferred_element_type=F32) + bvt_ref[...]
    vt_ref[0] = vt.astype(BF16)
    qm = jnp.dot(xb, wm_ref[...], preferred_element_type=F32) + bm_ref[...]
    qm_ref[0] = (qm * qm_scale).astype(BF16)

    w = cw_ref[...]
    for r in range(0, tm, rows):
        acc = jnp.zeros((rows, ch), F32)
        for t in range(CONV_WIDTH):
            tiles, ph = divmod(HALO - CONV_PAD + t, SUBLANES)
            start = r + tiles * SUBLANES
            acc = acc + us_ref[ph, start:start + rows, :] * w[t:t + 1, :]
        y = _layer_norm(acc + cb_ref[...], cg_ref[...], cbeta_ref[...])
        conv_ref[0, r:r + rows, :] = (y * jax.nn.sigmoid(y)).astype(BF16)


def _inproj_conv(x, w_in, b_in, conv_w, conv_b, conv_ln_g, conv_ln_b, dims, tm=512, rows=64):
    bsz, s, d = x.shape
    conv_ch, qk_w, diff_w, mem_w = dims
    s0 = 2 * conv_ch
    s1 = s0 + qk_w
    s2 = s1 + qk_w
    s3 = s2 + diff_w
    wb = w_in.astype(BF16)
    wc, wq, wk, wv, wm = wb[:, :s0], wb[:, s0:s1], wb[:, s1:s2], wb[:, s2:s3], wb[:, s3:]
    bc, bq, bk, bv, bm = b_in[:s0], b_in[s0:s1], b_in[s1:s2], b_in[s2:s3], b_in[s3:]
    dh = qk_w // (2 * DIFF_HEADS)
    mem_dh = mem_w // MEM_HEADS
    hb = tm // HALO
    nh = s // HALO
    kern = functools.partial(_inproj_conv_kernel, q_scale=dh ** -0.5 * LOG2E, qm_scale=mem_dh ** -0.5,
                             ch=conv_ch, rows=rows)
    const = lambda shape: pl.BlockSpec(shape, lambda b, i: (0, 0))
    return pl.pallas_call(
        kern,
        grid=(bsz, s // tm),
        in_specs=[pl.BlockSpec((1, tm, d), lambda b, i: (b, i, 0)),
                  pl.BlockSpec((1, HALO, d), lambda b, i: (b, jnp.maximum(i * hb - 1, 0), 0)),
                  pl.BlockSpec((1, HALO, d), lambda b, i: (b, jnp.minimum((i + 1) * hb, nh - 1), 0)),
                  const((d, s0)), const((d, qk_w)), const((qk_w, d)), const((diff_w, d)), const((d, mem_w)),
                  const((1, s0)), const((1, qk_w)), const((qk_w, 1)), const((diff_w, 1)), const((1, mem_w)),
                  const((CONV_WIDTH, conv_ch)), const((1, conv_ch)), const((1, conv_ch)), const((1, conv_ch))],
        out_specs=[pl.BlockSpec((1, tm, conv_ch), lambda b, i: (b, i, 0)),
                   pl.BlockSpec((1, tm, qk_w), lambda b, i: (b, i, 0)),
                   pl.BlockSpec((1, qk_w, tm), lambda b, i: (b, 0, i)),
                   pl.BlockSpec((1, diff_w, tm), lambda b, i: (b, 0, i)),
                   pl.BlockSpec((1, tm, mem_w), lambda b, i: (b, i, 0))],
        out_shape=[jax.ShapeDtypeStruct((bsz, s, conv_ch), BF16),
                   jax.ShapeDtypeStruct((bsz, s, qk_w), BF16),
                   jax.ShapeDtypeStruct((bsz, qk_w, s), BF16),
                   jax.ShapeDtypeStruct((bsz, diff_w, s), BF16),
                   jax.ShapeDtypeStruct((bsz, s, mem_w), BF16)],
        scratch_shapes=[pltpu.VMEM((tm + 2 * HALO, conv_ch), F32),
                        pltpu.VMEM((SUBLANES, tm + 2 * HALO - SUBLANES, conv_ch), F32)],
        compiler_params=_cparams("parallel", "parallel"),
        name="inproj_conv",
    )(x, x, x, wc, wk, wq.T, wv.T, wm,
      bc.reshape(1, -1), bk.reshape(1, -1), bq.reshape(-1, 1), bv.reshape(-1, 1), bm.reshape(1, -1),
      conv_w, conv_b.reshape(1, -1), conv_ln_g.reshape(1, -1), conv_ln_b.reshape(1, -1))


def _memkv_kernel(mem_ref, wkt_ref, wv_ref, kbd_ref, vbd_ref, *, heads):
    mb = mem_ref[0].astype(BF16)
    kmt = lax.dot_general(wkt_ref[...], mb, _NT, preferred_element_type=F32)
    vm = jnp.dot(mb, wv_ref[...], preferred_element_type=F32)
    w, m = kmt.shape
    dh = w // heads
    row_head = lax.broadcasted_iota(jnp.int32, (w, m), 0) // dh
    col_head = lax.broadcasted_iota(jnp.int32, (m, w), 1) // dh
    for h in range(heads):
        kbd_ref[0, :, h * m:(h + 1) * m] = jnp.where(row_head == h, kmt, 0.0).astype(BF16)
        vbd_ref[0, h * m:(h + 1) * m, :] = jnp.where(col_head == h, vm, 0.0).astype(BF16)


def _memkv(mem, w_mem_kv, mem_w):
    bsz, m, d = mem.shape
    wb = w_mem_kv.astype(BF16)
    wkt, wv = wb[:, :mem_w].T, wb[:, mem_w:]
    kern = functools.partial(_memkv_kernel, heads=MEM_HEADS)
    return pl.pallas_call(
        kern,
        grid=(bsz,),
        in_specs=[pl.BlockSpec((1, m, d), lambda b: (b, 0, 0)),
                  pl.BlockSpec((mem_w, d), lambda b: (0, 0)),
                  pl.BlockSpec((d, mem_w), lambda b: (0, 0))],
        out_specs=[pl.BlockSpec((1, mem_w, MEM_HEADS * m), lambda b: (b, 0, 0)),
                   pl.BlockSpec((1, MEM_HEADS * m, mem_w), lambda b: (b, 0, 0))],
        out_shape=[jax.ShapeDtypeStruct((bsz, mem_w, MEM_HEADS * m), BF16),
                   jax.ShapeDtypeStruct((bsz, MEM_HEADS * m, mem_w), BF16)],
        compiler_params=_cparams("parallel"),
        name="mem_kv",
    )(mem, wkt, wv)


def _memattn_kernel(qm_ref, kbd_ref, vbd_ref, o_ref, *, heads):
    s = jnp.dot(qm_ref[0], kbd_ref[0], preferred_element_type=F32)
    m = s.shape[1] // heads
    w = o_ref.shape[2]
    dh = w // heads
    lane_head = lax.broadcasted_iota(jnp.int32, (1, w), 1) // dh
    ps = []
    scale = jnp.zeros((s.shape[0], w), F32)
    for h in range(heads):
        sh = s[:, h * m:(h + 1) * m]
        p = jnp.exp(sh - jnp.max(sh, axis=-1, keepdims=True))
        scale = jnp.where(lane_head == h, 1.0 / jnp.sum(p, axis=-1, keepdims=True), scale)
        ps.append(p.astype(BF16))
    o = jnp.dot(jnp.concatenate(ps, axis=1), vbd_ref[0], preferred_element_type=F32)
    o_ref[0] = (o * scale).astype(BF16)


def _memattn(qm, kbd, vbd, tq=512):
    bsz, s, w = qm.shape
    hm = kbd.shape[2]
    kern = functools.partial(_memattn_kernel, heads=MEM_HEADS)
    return pl.pallas_call(
        kern,
        grid=(bsz, s // tq),
        in_specs=[pl.BlockSpec((1, tq, w), lambda b, i: (b, i, 0)),
                  pl.BlockSpec((1, w, hm), lambda b, i: (b, 0, 0)),
                  pl.BlockSpec((1, hm, w), lambda b, i: (b, 0, 0))],
        out_specs=pl.BlockSpec((1, tq, w), lambda b, i: (b, i, 0)),
        out_shape=jax.ShapeDtypeStruct((bsz, s, w), BF16),
        compiler_params=_cparams("parallel", "parallel"),
        name="mem_attn",
    )(qm, kbd, vbd)


def _t5_bucket(rel):
    half = N_BUCKETS // 2
    max_exact = half // 2
    ret = (rel > 0).astype(jnp.int32) * half
    n = jnp.abs(rel)
    nf = jnp.maximum(n, 1).astype(jnp.float32)
    large = max_exact + (jnp.log(nf / max_exact) / math.log(MAX_DISTANCE / max_exact)
                         * (half - max_exact)).astype(jnp.int32)
    large = jnp.minimum(large, half - 1)
    return ret + jnp.where(n < max_exact, n, large)


def _bias_tile_kernel(relb_ref, idx_ref, o_ref):
    h = pl.program_id(0)
    idx = idx_ref[0]
    acc = jnp.zeros(idx.shape, F32)
    for bkt in range(N_BUCKETS):
        acc = jnp.where(idx == bkt, relb_ref[bkt, h], acc)
    o_ref[0, 0] = acc * LOG2E


def _bias_tiles(rel_bias, s):
    t = ATT_TILE
    sat = (N_BUCKETS // 4) * (MAX_DISTANCE / (N_BUCKETS // 4)) ** ((N_BUCKETS // 2 - 1 - N_BUCKETS // 4) / (N_BUCKETS // 4))
    assert t + 1 > sat + 1, "attention tile too small for the saturated-bias shortcut"
    kk = jnp.arange(t, dtype=jnp.int32)[:, None]
    qq = jnp.arange(t, dtype=jnp.int32)[None, :]
    idx = jnp.stack([_t5_bucket((d - 1) * t + kk - qq) for d in range(3)])
    far = _t5_bucket(jnp.array([-(s - 1), s - 1], jnp.int32))
    tiles = pl.pallas_call(
        _bias_tile_kernel,
        grid=(DIFF_HEADS, 3),
        in_specs=[pl.BlockSpec(memory_space=pltpu.SMEM),
                  pl.BlockSpec((1, t, t), lambda h, d: (d, 0, 0))],
        out_specs=pl.BlockSpec((1, 1, t, t), lambda h, d: (h, d, 0, 0)),
        out_shape=jax.ShapeDtypeStruct((DIFF_HEADS, 3, t, t), F32),
        compiler_params=_cparams("parallel", "parallel"),
        name="bias_tiles",
    )(rel_bias, idx)
    return tiles, far


def _diffattn_kernel(far_ref, relb_ref, lami_ref, qt_ref, k_ref, vt_ref, bias_ref, g_ref,
                     lq1_ref, lk1_ref, lq2_ref, lk2_ref, o_ref, vaug_ref, s_ref, p_ref, *, dh, dv, nk):
    t = ATT_TILE
    h = pl.program_id(1)
    pair = pl.program_id(2)

    @pl.when(pair == 0)
    def _():
        vaug_ref[0:dv, :] = vt_ref[0]
        vaug_ref[dv:, :] = jnp.ones((BF16_ROWS, vaug_ref.shape[1]), BF16)

    c_left = relb_ref[far_ref[0], h] * LOG2E
    c_right = relb_ref[far_ref[1], h] * LOG2E
    row = lax.broadcasted_iota(jnp.int32, (2 * dh, t), 0)
    qis, qtms = [], []
    for u in range(QT_PER_STEP):
        qt = qt_ref[0, :, u * t:(u + 1) * t]
        qis.append(pair * QT_PER_STEP + u)
        qtms.append((jnp.where(row < dh, qt, jnp.zeros_like(qt)), jnp.where(row >= dh, qt, jnp.zeros_like(qt))))
    m8 = [[jnp.full((SUBLANES, t), -jnp.inf, F32) for _ in range(2)] for _ in range(QT_PER_STEP)]
    mrow = [[None, None] for _ in range(QT_PER_STEP)]

    def colmax(s):
        return jnp.max(s.reshape(t // SUBLANES, SUBLANES, t), axis=0)

    def far_bias(u, j, near):
        return jnp.where(j < qis[u] - 1, c_left, jnp.where(j > qis[u] + 1, c_right, near))

    def score_tile(u, j):
        c_far = far_bias(u, j, -jnp.inf)
        for mp in range(2):
            s = jnp.dot(k_ref[0, j * t:(j + 1) * t, :], qtms[u][mp], preferred_element_type=F32)
            s_ref[u % 2, mp, j * t:(j + 1) * t, :] = s
            m8[u][mp] = jnp.maximum(m8[u][mp], colmax(s) + c_far)

    def finish_scores(u):
        for d in range(3):
            j = qis[u] - 1 + d
            valid = jnp.logical_and(j >= 0, j < nk)
            rows = pl.ds(pl.multiple_of(jnp.clip(j, 0, nk - 1) * t, t), t)
            bias = jnp.where(valid, bias_ref[0, d], 0.0)
            for mp in range(2):
                s = s_ref[u % 2, mp, rows, :] + bias
                s_ref[u % 2, mp, rows, :] = s
                m8[u][mp] = jnp.maximum(m8[u][mp], jnp.where(valid, colmax(s), -jnp.inf))
        for mp in range(2):
            mrow[u][mp] = jnp.max(m8[u][mp], axis=0, keepdims=True)

    def exp_tile(u, j):
        c = far_bias(u, j, 0.0)
        for mp in range(2):
            p = jnp.exp2(s_ref[u % 2, mp, j * t:(j + 1) * t, :] - (mrow[u][mp] - c))
            p_ref[u % 2, mp, j * t:(j + 1) * t, :] = p.astype(BF16)

    def value_chunk(u, c, acc):
        keys = slice(c * PV_CHUNK * t, (c + 1) * PV_CHUNK * t)
        out = []
        for mp in range(2):
            part = jnp.dot(vaug_ref[:, keys], p_ref[u % 2, mp, keys, :], preferred_element_type=F32)
            out.append(part if acc is None else acc[mp] + part)
        return out

    def finish(u, acc):
        o1 = acc[0][0:dv] / acc[0][dv:dv + 1]
        o2 = acc[1][0:dv] / acc[1][dv:dv + 1]
        lam_init = lami_ref[0]
        e1 = jnp.exp(jnp.sum(lq1_ref[...] * lk1_ref[...], axis=-1, keepdims=True))
        e2 = jnp.exp(jnp.sum(lq2_ref[...] * lk2_ref[...], axis=-1, keepdims=True))
        lam = e1 - e2 + lam_init
        ot = o1 - lam * o2
        ms = jnp.mean(ot * ot, axis=0, keepdims=True)
        ot = ot * lax.rsqrt(ms + LN_EPS) * g_ref[...] * (1.0 - lam_init)
        o_ref[0, u * t:(u + 1) * t, :] = ot.T.astype(BF16)

    for ph in range(QT_PER_STEP + 2):
        acc = None
        for j in range(nk):
            if 1 <= ph <= QT_PER_STEP:
                exp_tile(ph - 1, j)
            if ph < QT_PER_STEP:
                score_tile(ph, j)
            if ph >= 2 and j % PV_CHUNK == 0:
                acc = value_chunk(ph - 2, j // PV_CHUNK, acc)
        if ph < QT_PER_STEP:
            finish_scores(ph)
        if ph >= 2:
            finish(ph - 2, acc)


def _diffattn(qt, k, vt, bias_tiles, far, rel_bias, lam_init, norm_g, lq1, lk1, lq2, lk2):
    bsz, s, qk_w = k.shape
    diff_w = vt.shape[1]
    dv = diff_w // DIFF_HEADS
    dh = qk_w // (2 * DIFF_HEADS)
    t = ATT_TILE
    nk = s // t
    tq = QT_PER_STEP * t
    kern = functools.partial(_diffattn_kernel, dh=dh, dv=dv, nk=nk)
    smem = pl.BlockSpec(memory_space=pltpu.SMEM)
    lvec = lambda: pl.BlockSpec((1, dh), lambda b, h, i: (0, 0))
    return pl.pallas_call(
        kern,
        grid=(bsz, DIFF_HEADS, s // tq),
        in_specs=[smem, smem, smem,
                  pl.BlockSpec((1, 2 * dh, tq), lambda b, h, i: (b, h, i)),
                  pl.BlockSpec((1, s, 2 * dh), lambda b, h, i: (b, 0, h)),
                  pl.BlockSpec((1, dv, s), lambda b, h, i: (b, h, 0)),
                  pl.BlockSpec((1, 3, t, t), lambda b, h, i: (h, 0, 0, 0)),
                  pl.BlockSpec((dv, 1), lambda b, h, i: (0, 0)),
                  lvec(), lvec(), lvec(), lvec()],
        out_specs=pl.BlockSpec((1, tq, dv), lambda b, h, i: (b, i, h)),
        out_shape=jax.ShapeDtypeStruct((bsz, s, diff_w), BF16),
        scratch_shapes=[pltpu.VMEM((dv + BF16_ROWS, s), BF16),
                        pltpu.VMEM((2, 2, s, t), F32),
                        pltpu.VMEM((2, 2, s, t), BF16)],
        compiler_params=_cparams("parallel", "parallel", "arbitrary"),
        name="diff_attn",
    )(far, rel_bias, jnp.full((1,), lam_init, F32), qt, k, vt, bias_tiles, norm_g.reshape(dv, 1),
      lq1.reshape(1, dh), lk1.reshape(1, dh), lq2.reshape(1, dh), lk2.reshape(1, dh))


def _outproj_kernel(x_ref, c_ref, d_ref, m_ref, w_ref, b_ref, g_ref, beta_ref, o_ref, *, rows):
    for r in range(0, x_ref.shape[0], rows):
        sl = slice(r, r + rows)
        mixed = jnp.concatenate([c_ref[sl, :], d_ref[sl, :], m_ref[sl, :]], axis=1)
        y = jnp.dot(mixed, w_ref[...], preferred_element_type=F32)
        z = ALPHA * x_ref[sl, :] + (y + b_ref[...])
        o_ref[sl, :] = _layer_norm(z, g_ref[...], beta_ref[...])


def _outproj_ln(x2d, conv_out, diff_out, mem_out, w_out, b_out, g, beta, tm=512, rows=128):
    m, d = x2d.shape
    cw, dw, mw = conv_out.shape[1], diff_out.shape[1], mem_out.shape[1]
    kern = functools.partial(_outproj_kernel, rows=rows)
    rowblk = lambda w: pl.BlockSpec((tm, w), lambda i: (i, 0))
    const = lambda shape: pl.BlockSpec(shape, lambda i: (0, 0))
    return pl.pallas_call(
        kern,
        grid=(m // tm,),
        in_specs=[rowblk(d), rowblk(cw), rowblk(dw), rowblk(mw),
                  const((cw + dw + mw, d)),
                  const((1, d)), const((1, d)), const((1, d))],
        out_specs=rowblk(d),
        out_shape=jax.ShapeDtypeStruct((m, d), F32),
        compiler_params=_cparams("parallel"),
        name="outproj_ln",
    )(x2d, conv_out, diff_out, mem_out, w_out.astype(BF16),
      b_out.reshape(1, d), g.reshape(1, d), beta.reshape(1, d))


def _mlp_kernel(x_ref, wu_ref, wd_ref, g_ref, beta_ref, o_ref, *, chunk, rows):
    for r in range(0, x_ref.shape[0], rows):
        x = x_ref[r:r + rows, :]
        xb = x.astype(BF16)
        ff = jnp.zeros(x.shape, F32)
        for c in range(0, wu_ref.shape[1], chunk):
            hcol = jnp.maximum(jnp.dot(xb, wu_ref[:, c:c + chunk], preferred_element_type=F32), 0.0)
            ff = ff + jnp.dot((hcol * hcol).astype(BF16), wd_ref[c:c + chunk, :], preferred_element_type=F32)
        o_ref[r:r + rows, :] = _layer_norm(ALPHA * x + ff, g_ref[...], beta_ref[...])


def _mlp_ln(x2d, w_up, w_down, g, beta, tm=512, chunk=1024, rows=256):
    m, d = x2d.shape
    dff = w_up.shape[1]
    kern = functools.partial(_mlp_kernel, chunk=chunk, rows=rows)
    const = lambda shape: pl.BlockSpec(shape, lambda i: (0, 0))
    return pl.pallas_call(
        kern,
        grid=(m // tm,),
        in_specs=[pl.BlockSpec((tm, d), lambda i: (i, 0)),
                  const((d, dff)), const((dff, d)), const((1, d)), const((1, d))],
        out_specs=pl.BlockSpec((tm, d), lambda i: (i, 0)),
        out_shape=jax.ShapeDtypeStruct((m, d), F32),
        compiler_params=_cparams("parallel"),
        name="mlp_ln",
    )(x2d, w_up.astype(BF16), w_down.astype(BF16), g.reshape(1, d), beta.reshape(1, d))


def kernel(x, mem, emb_ln_g, emb_ln_b, rel_bias, w_in, b_in, conv_w, conv_b, conv_ln_g, conv_ln_b,
           lambda_q1, lambda_k1, lambda_q2, lambda_k2, diff_norm_g, w_mem_kv, w_out, b_out,
           ln1_g, ln1_b, w_up, w_down, ln2_g, ln2_b):
    bsz, s, d = x.shape
    conv_ch = conv_w.shape[2]
    dh = lambda_q1.shape[1]
    qk_w = DIFF_HEADS * 2 * dh
    diff_w = DIFF_HEADS * diff_norm_g.shape[1]
    mem_w = w_mem_kv.shape[2] // 2
    dims = (conv_ch, qk_w, diff_w, mem_w)
    assert s % (QT_PER_STEP * ATT_TILE) == 0 and (s // ATT_TILE) % PV_CHUNK == 0 and 2 * dh == LANES and diff_norm_g.shape[1] == LANES

    xs = _embed_ln(x.reshape(bsz * s, d), emb_ln_g, emb_ln_b)
    bias_tiles, far = _bias_tiles(rel_bias, s)
    for l in range(DEPTH):
        lam_init = 0.8 - 0.6 * math.exp(-0.3 * l)
        conv_out, k, qt, vt, qm = _inproj_conv(xs.reshape(bsz, s, d), w_in[l], b_in[l], conv_w[l], conv_b[l],
                                               conv_ln_g[l], conv_ln_b[l], dims)
        diff_out = _diffattn(qt, k, vt, bias_tiles, far, rel_bias, lam_init, diff_norm_g[l],
                             lambda_q1[l], lambda_k1[l], lambda_q2[l], lambda_k2[l])
        kbd, vbd = _memkv(mem, w_mem_kv[l], mem_w)
        mem_out = _memattn(qm, kbd, vbd)
        xs = _outproj_ln(xs, conv_out.reshape(bsz * s, -1), diff_out.reshape(bsz * s, -1),
                         mem_out.reshape(bsz * s, -1), w_out[l], b_out[l], ln1_g[l], ln1_b[l])
        xs = _mlp_ln(xs, w_up[l], w_down[l], ln2_g[l], ln2_b[l])
    return xs.reshape(bsz, s, d)
```

```python
import functools
import math

import jax
import jax.numpy as jnp
from jax import lax
from jax.experimental import pallas as pl
from jax.experimental.pallas import tpu as pltpu

F32 = jnp.float32
BF16 = jnp.bfloat16

DEPTH = 4
CONV_WIDTH = 31
CONV_PAD = (CONV_WIDTH - 1) // 2
DIFF_HEADS = 4
MEM_HEADS = 4
N_BUCKETS = 32
MAX_DISTANCE = 128
ALPHA = (2.0 * DEPTH) ** 0.25
LN_EPS = 1e-5
LOG2E = math.log2(math.e)

LANES = 128
SUBLANES = 8
BF16_ROWS = 16
MXU_DIM = 256
VMEM_LIMIT = 56 * 1024 * 1024

ATT_TILE = MXU_DIM
QT_PER_STEP = 2
PV_CHUNK = 4
HALO = 16

_NT = (((1,), (1,)), ((), ()))


def _cparams(*sem):
    return pltpu.CompilerParams(dimension_semantics=sem, vmem_limit_bytes=VMEM_LIMIT)


def _layer_norm(z, g, b):
    mu = jnp.mean(z, axis=-1, keepdims=True)
    zc = z - mu
    var = jnp.mean(zc * zc, axis=-1, keepdims=True)
    return zc * lax.rsqrt(var + LN_EPS) * g + b


def _ln_kernel(x_ref, g_ref, b_ref, o_ref):
    o_ref[...] = _layer_norm(x_ref[...], g_ref[...], b_ref[...])


def _embed_ln(x2d, g, b, tm=1024):
    m, d = x2d.shape
    return pl.pallas_call(
        _ln_kernel,
        grid=(m // tm,),
        in_specs=[pl.BlockSpec((tm, d), lambda i: (i, 0)),
                  pl.BlockSpec((1, d), lambda i: (0, 0)),
                  pl.BlockSpec((1, d), lambda i: (0, 0))],
        out_specs=pl.BlockSpec((tm, d), lambda i: (i, 0)),
        out_shape=jax.ShapeDtypeStruct((m, d), F32),
        compiler_params=_cparams("parallel"),
        name="embed_ln",
    )(x2d, g.reshape(1, d), b.reshape(1, d))


def _inproj_conv_kernel(x_ref, xp_ref, xn_ref, wc_ref, wk_ref, wqt_ref, wvt_ref, wm_ref,
                        bc_ref, bk_ref, bqt_ref, bvt_ref, bm_ref, cw_ref, cb_ref, cg_ref, cbeta_ref,
                        conv_ref, k_ref, qt_ref, vt_ref, qm_ref, u_ref, us_ref, *, q_scale, qm_scale, ch, rows):
    i = pl.program_id(1)
    n = pl.num_programs(1)
    tm = x_ref.shape[1]
    xb = x_ref[0].astype(BF16)

    def glu(c):
        return c[:, :ch] * jax.nn.sigmoid(c[:, ch:])

    halo = jnp.concatenate([xp_ref[0], xn_ref[0]], axis=0).astype(BF16)
    c_halo = jnp.dot(halo, wc_ref[...], preferred_element_type=F32) + bc_ref[...]
    c_cur = jnp.dot(xb, wc_ref[...], preferred_element_type=F32) + bc_ref[...]
    u_ref[0:HALO, :] = jnp.where(i > 0, glu(c_halo[:HALO]), 0.0)
    u_ref[HALO:HALO + tm, :] = glu(c_cur)
    u_ref[HALO + tm:, :] = jnp.where(i < n - 1, glu(c_halo[HALO:]), 0.0)
    for ph in range(SUBLANES):
        us_ref[ph] = u_ref[ph:ph + us_ref.shape[1], :]

    k_ref[0] = (jnp.dot(xb, wk_ref[...], preferred_element_type=F32) + bk_ref[...]).astype(BF16)
    qt = lax.dot_general(wqt_ref[...], xb, _NT, preferred_element_type=F32) + bqt_ref[...]
    qt_ref[0] = (qt * q_scale).astype(BF16)
    vt = lax.dot_general(wvt_ref[...], xb, _NT, preferred_element_type=F32) + bvt_ref[...]
    vt_ref[0] = vt.astype(BF16)
    qm = jnp.dot(xb, wm_ref[...], preferred_element_type=F32) + bm_ref[...]
    qm_ref[0] = (qm * qm_scale).astype(BF16)

    w = cw_ref[...]
    for r in range(0, tm, rows):
        acc = jnp.zeros((rows, ch), F32)
        for t in range(CONV_WIDTH):
            tiles, ph = divmod(HALO - CONV_PAD + t, SUBLANES)
            start = r + tiles * SUBLANES
            acc = acc + us_ref[ph, start:start + rows, :] * w[t:t + 1, :]
        y = _layer_norm(acc + cb_ref[...], cg_ref[...], cbeta_ref[...])
        conv_ref[0, r:r + rows, :] = (y * jax.nn.sigmoid(y)).astype(BF16)


def _inproj_conv(x, w_in, b_in, conv_w, conv_b, conv_ln_g, conv_ln_b, dims, tm=512, rows=64):
    bsz, s, d = x.shape
    conv_ch, qk_w, diff_w, mem_w = dims
    s0 = 2 * conv_ch
    s1 = s0 + qk_w
    s2 = s1 + qk_w
    s3 = s2 + diff_w
    wb = w_in.astype(BF16)
    wc, wq, wk, wv, wm = wb[:, :s0], wb[:, s0:s1], wb[:, s1:s2], wb[:, s2:s3], wb[:, s3:]
    bc, bq, bk, bv, bm = b_in[:s0], b_in[s0:s1], b_in[s1:s2], b_in[s2:s3], b_in[s3:]
    dh = qk_w // (2 * DIFF_HEADS)
    mem_dh = mem_w // MEM_HEADS
    hb = tm // HALO
    nh = s // HALO
    kern = functools.partial(_inproj_conv_kernel, q_scale=dh ** -0.5 * LOG2E, qm_scale=mem_dh ** -0.5,
                             ch=conv_ch, rows=rows)
    const = lambda shape: pl.BlockSpec(shape, lambda b, i: (0, 0))
    return pl.pallas_call(
        kern,
        grid=(bsz, s // tm),
        in_specs=[pl.BlockSpec((1, tm, d), lambda b, i: (b, i, 0)),
                  pl.BlockSpec((1, HALO, d), lambda b, i: (b, jnp.maximum(i * hb - 1, 0), 0)),
                  pl.BlockSpec((1, HALO, d), lambda b, i: (b, jnp.minimum((i + 1) * hb, nh - 1), 0)),
                  const((d, s0)), const((d, qk_w)), const((qk_w, d)), const((diff_w, d)), const((d, mem_w)),
                  const((1, s0)), const((1, qk_w)), const((qk_w, 1)), const((diff_w, 1)), const((1, mem_w)),
                  const((CONV_WIDTH, conv_ch)), const((1, conv_ch)), const((1, conv_ch)), const((1, conv_ch))],
        out_specs=[pl.BlockSpec((1, tm, conv_ch), lambda b, i: (b, i, 0)),
                   pl.BlockSpec((1, tm, qk_w), lambda b, i: (b, i, 0)),
                   pl.BlockSpec((1, qk_w, tm), lambda b, i: (b, 0, i)),
                   pl.BlockSpec((1, diff_w, tm), lambda b, i: (b, 0, i)),
                   pl.BlockSpec((1, tm, mem_w), lambda b, i: (b, i, 0))],
        out_shape=[jax.ShapeDtypeStruct((bsz, s, conv_ch), BF16),
                   jax.ShapeDtypeStruct((bsz, s, qk_w), BF16),
                   jax.ShapeDtypeStruct((bsz, qk_w, s), BF16),
                   jax.ShapeDtypeStruct((bsz, diff_w, s), BF16),
                   jax.ShapeDtypeStruct((bsz, s, mem_w), BF16)],
        scratch_shapes=[pltpu.VMEM((tm + 2 * HALO, conv_ch), F32),
                        pltpu.VMEM((SUBLANES, tm + 2 * HALO - SUBLANES, conv_ch), F32)],
        compiler_params=_cparams("parallel", "parallel"),
        name="inproj_conv",
    )(x, x, x, wc, wk, wq.T, wv.T, wm,
      bc.reshape(1, -1), bk.reshape(1, -1), bq.reshape(-1, 1), bv.reshape(-1, 1), bm.reshape(1, -1),
      conv_w, conv_b.reshape(1, -1), conv_ln_g.reshape(1, -1), conv_ln_b.reshape(1, -1))


def _memkv_kernel(mem_ref, wkt_ref, wv_ref, kbd_ref, vbd_ref, *, heads):
    mb = mem_ref[0].astype(BF16)
    kmt = lax.dot_general(wkt_ref[...], mb, _NT, preferred_element_type=F32)
    vm = jnp.dot(mb, wv_ref[...], preferred_element_type=F32)
    w, m = kmt.shape
    dh = w // heads
    row_head = lax.broadcasted_iota(jnp.int32, (w, m), 0) // dh
    col_head = lax.broadcasted_iota(jnp.int32, (m, w), 1) // dh
    for h in range(heads):
        kbd_ref[0, :, h * m:(h + 1) * m] = jnp.where(row_head == h, kmt, 0.0).astype(BF16)
        vbd_ref[0, h * m:(h + 1) * m, :] = jnp.where(col_head == h, vm, 0.0).astype(BF16)


def _memkv(mem, w_mem_kv, mem_w):
    bsz, m, d = mem.shape
    wb = w_mem_kv.astype(BF16)
    wkt, wv = wb[:, :mem_w].T, wb[:, mem_w:]
    kern = functools.partial(_memkv_kernel, heads=MEM_HEADS)
    return pl.pallas_call(
        kern,
        grid=(bsz,),
        in_specs=[pl.BlockSpec((1, m, d), lambda b: (b, 0, 0)),
                  pl.BlockSpec((mem_w, d), lambda b: (0, 0)),
                  pl.BlockSpec((d, mem_w), lambda b: (0, 0))],
        out_specs=[pl.BlockSpec((1, mem_w, MEM_HEADS * m), lambda b: (b, 0, 0)),
                   pl.BlockSpec((1, MEM_HEADS * m, mem_w), lambda b: (b, 0, 0))],
        out_shape=[jax.ShapeDtypeStruct((bsz, mem_w, MEM_HEADS * m), BF16),
                   jax.ShapeDtypeStruct((bsz, MEM_HEADS * m, mem_w), BF16)],
        compiler_params=_cparams("parallel"),
        name="mem_kv",
    )(mem, wkt, wv)


def _memattn_rows(qm, kbd, vbd, heads):
    s = jnp.dot(qm, kbd, preferred_element_type=F32)
    m = s.shape[1] // heads
    w = vbd.shape[1]
    dh = w // heads
    lane_head = lax.broadcasted_iota(jnp.int32, (1, w), 1) // dh
    ps = []
    scale = jnp.zeros((s.shape[0], w), F32)
    for h in range(heads):
        sh = s[:, h * m:(h + 1) * m]
        p = jnp.exp(sh - jnp.max(sh, axis=-1, keepdims=True))
        scale = jnp.where(lane_head == h, 1.0 / jnp.sum(p, axis=-1, keepdims=True), scale)
        ps.append(p.astype(BF16))
    o = jnp.dot(jnp.concatenate(ps, axis=1), vbd, preferred_element_type=F32)
    return (o * scale).astype(BF16)


def _t5_bucket(rel):
    half = N_BUCKETS // 2
    max_exact = half // 2
    ret = (rel > 0).astype(jnp.int32) * half
    n = jnp.abs(rel)
    nf = jnp.maximum(n, 1).astype(jnp.float32)
    large = max_exact + (jnp.log(nf / max_exact) / math.log(MAX_DISTANCE / max_exact)
                         * (half - max_exact)).astype(jnp.int32)
    large = jnp.minimum(large, half - 1)
    return ret + jnp.where(n < max_exact, n, large)


def _bias_tile_kernel(relb_ref, idx_ref, o_ref):
    h = pl.program_id(0)
    idx = idx_ref[0]
    acc = jnp.zeros(idx.shape, F32)
    for bkt in range(N_BUCKETS):
        acc = jnp.where(idx == bkt, relb_ref[bkt, h], acc)
    o_ref[0, 0] = acc * LOG2E


def _bias_tiles(rel_bias, s):
    t = ATT_TILE
    sat = (N_BUCKETS // 4) * (MAX_DISTANCE / (N_BUCKETS // 4)) ** ((N_BUCKETS // 2 - 1 - N_BUCKETS // 4) / (N_BUCKETS // 4))
    assert t + 1 > sat + 1, "attention tile too small for the saturated-bias shortcut"
    kk = jnp.arange(t, dtype=jnp.int32)[:, None]
    qq = jnp.arange(t, dtype=jnp.int32)[None, :]
    idx = jnp.stack([_t5_bucket((d - 1) * t + kk - qq) for d in range(3)])
    far = _t5_bucket(jnp.array([-(s - 1), s - 1], jnp.int32))
    tiles = pl.pallas_call(
        _bias_tile_kernel,
        grid=(DIFF_HEADS, 3),
        in_specs=[pl.BlockSpec(memory_space=pltpu.SMEM),
                  pl.BlockSpec((1, t, t), lambda h, d: (d, 0, 0))],
        out_specs=pl.BlockSpec((1, 1, t, t), lambda h, d: (h, d, 0, 0)),
        out_shape=jax.ShapeDtypeStruct((DIFF_HEADS, 3, t, t), F32),
        compiler_params=_cparams("parallel", "parallel"),
        name="bias_tiles",
    )(rel_bias, idx)
    return tiles, far


def _diffattn_kernel(far_ref, relb_ref, lami_ref, qt_ref, k_ref, vt_ref, bias_ref, g_ref,
                     lq1_ref, lk1_ref, lq2_ref, lk2_ref, o_ref, vaug_ref, s_ref, p_ref, *, dh, dv, nk):
    t = ATT_TILE
    h = pl.program_id(1)
    pair = pl.program_id(2)

    @pl.when(pair == 0)
    def _():
        vaug_ref[0:dv, :] = vt_ref[0]
        vaug_ref[dv:, :] = jnp.ones((BF16_ROWS, vaug_ref.shape[1]), BF16)

    c_left = relb_ref[far_ref[0], h] * LOG2E
    c_right = relb_ref[far_ref[1], h] * LOG2E
    row = lax.broadcasted_iota(jnp.int32, (2 * dh, t), 0)
    qis, qtms = [], []
    for u in range(QT_PER_STEP):
        qt = qt_ref[0, :, u * t:(u + 1) * t]
        qis.append(pair * QT_PER_STEP + u)
        qtms.append((jnp.where(row < dh, qt, jnp.zeros_like(qt)), jnp.where(row >= dh, qt, jnp.zeros_like(qt))))
    m8 = [[jnp.full((SUBLANES, t), -jnp.inf, F32) for _ in range(2)] for _ in range(QT_PER_STEP)]
    mrow = [[None, None] for _ in range(QT_PER_STEP)]

    def colmax(s):
        return jnp.max(s.reshape(t // SUBLANES, SUBLANES, t), axis=0)

    def far_bias(u, j, near):
        return jnp.where(j < qis[u] - 1, c_left, jnp.where(j > qis[u] + 1, c_right, near))

    def score_tile(u, j):
        c_far = far_bias(u, j, -jnp.inf)
        for mp in range(2):
            s = jnp.dot(k_ref[0, j * t:(j + 1) * t, :], qtms[u][mp], preferred_element_type=F32)
            s_ref[u % 2, mp, j * t:(j + 1) * t, :] = s
            m8[u][mp] = jnp.maximum(m8[u][mp], colmax(s) + c_far)

    def finish_scores(u):
        for d in range(3):
            j = qis[u] - 1 + d
            valid = jnp.logical_and(j >= 0, j < nk)
            rows = pl.ds(pl.multiple_of(jnp.clip(j, 0, nk - 1) * t, t), t)
            bias = jnp.where(valid, bias_ref[0, d], 0.0)
            for mp in range(2):
                s = s_ref[u % 2, mp, rows, :] + bias
                s_ref[u % 2, mp, rows, :] = s
                m8[u][mp] = jnp.maximum(m8[u][mp], jnp.where(valid, colmax(s), -jnp.inf))
        for mp in range(2):
            mrow[u][mp] = jnp.max(m8[u][mp], axis=0, keepdims=True)

    def exp_tile(u, j):
        c = far_bias(u, j, 0.0)
        for mp in range(2):
            p = jnp.exp2(s_ref[u % 2, mp, j * t:(j + 1) * t, :] - (mrow[u][mp] - c))
            p_ref[u % 2, mp, j * t:(j + 1) * t, :] = p.astype(BF16)

    def value_chunk(u, c, acc):
        keys = slice(c * PV_CHUNK * t, (c + 1) * PV_CHUNK * t)
        out = []
        for mp in range(2):
            part = jnp.dot(vaug_ref[:, keys], p_ref[u % 2, mp, keys, :], preferred_element_type=F32)
            out.append(part if acc is None else acc[mp] + part)
        return out

    def finish(u, acc):
        o1 = acc[0][0:dv] / acc[0][dv:dv + 1]
        o2 = acc[1][0:dv] / acc[1][dv:dv + 1]
        lam_init = lami_ref[0]
        e1 = jnp.exp(jnp.sum(lq1_ref[...] * lk1_ref[...], axis=-1, keepdims=True))
        e2 = jnp.exp(jnp.sum(lq2_ref[...] * lk2_ref[...], axis=-1, keepdims=True))
        lam = e1 - e2 + lam_init
        ot = o1 - lam * o2
        ms = jnp.mean(ot * ot, axis=0, keepdims=True)
        ot = ot * lax.rsqrt(ms + LN_EPS) * g_ref[...] * (1.0 - lam_init)
        o_ref[0, u * t:(u + 1) * t, :] = ot.T.astype(BF16)

    for ph in range(QT_PER_STEP + 2):
        acc = None
        for j in range(nk):
            if 1 <= ph <= QT_PER_STEP:
                exp_tile(ph - 1, j)
            if ph < QT_PER_STEP:
                score_tile(ph, j)
            if ph >= 2 and j % PV_CHUNK == 0:
                acc = value_chunk(ph - 2, j // PV_CHUNK, acc)
        if ph < QT_PER_STEP:
            finish_scores(ph)
        if ph >= 2:
            finish(ph - 2, acc)


def _diffattn(qt, k, vt, bias_tiles, far, rel_bias, lam_init, norm_g, lq1, lk1, lq2, lk2):
    bsz, s, qk_w = k.shape
    diff_w = vt.shape[1]
    dv = diff_w // DIFF_HEADS
    dh = qk_w // (2 * DIFF_HEADS)
    t = ATT_TILE
    nk = s // t
    tq = QT_PER_STEP * t
    kern = functools.partial(_diffattn_kernel, dh=dh, dv=dv, nk=nk)
    smem = pl.BlockSpec(memory_space=pltpu.SMEM)
    lvec = lambda: pl.BlockSpec((1, dh), lambda b, h, i: (0, 0))
    return pl.pallas_call(
        kern,
        grid=(bsz, DIFF_HEADS, s // tq),
        in_specs=[smem, smem, smem,
                  pl.BlockSpec((1, 2 * dh, tq), lambda b, h, i: (b, h, i)),
                  pl.BlockSpec((1, s, 2 * dh), lambda b, h, i: (b, 0, h)),
                  pl.BlockSpec((1, dv, s), lambda b, h, i: (b, h, 0)),
                  pl.BlockSpec((1, 3, t, t), lambda b, h, i: (h, 0, 0, 0)),
                  pl.BlockSpec((dv, 1), lambda b, h, i: (0, 0)),
                  lvec(), lvec(), lvec(), lvec()],
        out_specs=pl.BlockSpec((1, tq, dv), lambda b, h, i: (b, i, h)),
        out_shape=jax.ShapeDtypeStruct((bsz, s, diff_w), BF16),
        scratch_shapes=[pltpu.VMEM((dv + BF16_ROWS, s), BF16),
                        pltpu.VMEM((2, 2, s, t), F32),
                        pltpu.VMEM((2, 2, s, t), BF16)],
        compiler_params=_cparams("parallel", "parallel", "arbitrary"),
        name="diff_attn",
    )(far, rel_bias, jnp.full((1,), lam_init, F32), qt, k, vt, bias_tiles, norm_g.reshape(dv, 1),
      lq1.reshape(1, dh), lk1.reshape(1, dh), lq2.reshape(1, dh), lk2.reshape(1, dh))


def _outproj_kernel(x_ref, c_ref, d_ref, qm_ref, kbd_ref, vbd_ref, w_ref, b_ref, g_ref, beta_ref, o_ref, *, rows):
    for r in range(0, x_ref.shape[1], rows):
        sl = slice(r, r + rows)
        mem_out = _memattn_rows(qm_ref[0, sl, :], kbd_ref[0], vbd_ref[0], MEM_HEADS)
        mixed = jnp.concatenate([c_ref[0, sl, :], d_ref[0, sl, :], mem_out], axis=1)
        y = jnp.dot(mixed, w_ref[...], preferred_element_type=F32)
        z = ALPHA * x_ref[0, sl, :] + (y + b_ref[...])
        o_ref[0, sl, :] = _layer_norm(z, g_ref[...], beta_ref[...])


def _outproj_ln(x, conv_out, diff_out, qm, kbd, vbd, w_out, b_out, g, beta, tm=512, rows=256):
    bsz, s, d = x.shape
    cw, dw, mw = conv_out.shape[2], diff_out.shape[2], qm.shape[2]
    hm = kbd.shape[2]
    kern = functools.partial(_outproj_kernel, rows=rows)
    rowblk = lambda w: pl.BlockSpec((1, tm, w), lambda b, i: (b, i, 0))
    const = lambda shape: pl.BlockSpec(shape, lambda b, i: (0, 0))
    return pl.pallas_call(
        kern,
        grid=(bsz, s // tm),
        in_specs=[rowblk(d), rowblk(cw), rowblk(dw), rowblk(mw),
                  pl.BlockSpec((1, mw, hm), lambda b, i: (b, 0, 0)),
                  pl.BlockSpec((1, hm, mw), lambda b, i: (b, 0, 0)),
                  const((cw + dw + mw, d)),
                  const((1, d)), const((1, d)), const((1, d))],
        out_specs=rowblk(d),
        out_shape=jax.ShapeDtypeStruct((bsz, s, d), F32),
        compiler_params=_cparams("parallel", "parallel"),
        name="outproj_ln",
    )(x, conv_out, diff_out, qm, kbd, vbd, w_out.astype(BF16),
      b_out.reshape(1, d), g.reshape(1, d), beta.reshape(1, d))


def _mlp_kernel(x_ref, wu_ref, wd_ref, g_ref, beta_ref, o_ref, *, chunk, rows):
    for r in range(0, x_ref.shape[0], rows):
        x = x_ref[r:r + rows, :]
        xb = x.astype(BF16)
        ff = jnp.zeros(x.shape, F32)
        for c in range(0, wu_ref.shape[1], chunk):
            hcol = jnp.maximum(jnp.dot(xb, wu_ref[:, c:c + chunk], preferred_element_type=F32), 0.0)
            ff = ff + jnp.dot((hcol * hcol).astype(BF16), wd_ref[c:c + chunk, :], preferred_element_type=F32)
        o_ref[r:r + rows, :] = _layer_norm(ALPHA * x + ff, g_ref[...], beta_ref[...])


def _mlp_ln(x2d, w_up, w_down, g, beta, tm=512, chunk=1024, rows=256):
    m, d = x2d.shape
    dff = w_up.shape[1]
    kern = functools.partial(_mlp_kernel, chunk=chunk, rows=rows)
    const = lambda shape: pl.BlockSpec(shape, lambda i: (0, 0))
    return pl.pallas_call(
        kern,
        grid=(m // tm,),
        in_specs=[pl.BlockSpec((tm, d), lambda i: (i, 0)),
                  const((d, dff)), const((dff, d)), const((1, d)), const((1, d))],
        out_specs=pl.BlockSpec((tm, d), lambda i: (i, 0)),
        out_shape=jax.ShapeDtypeStruct((m, d), F32),
        compiler_params=_cparams("parallel"),
        name="mlp_ln",
    )(x2d, w_up.astype(BF16), w_down.astype(BF16), g.reshape(1, d), beta.reshape(1, d))


def kernel(x, mem, emb_ln_g, emb_ln_b, rel_bias, w_in, b_in, conv_w, conv_b, conv_ln_g, conv_ln_b,
           lambda_q1, lambda_k1, lambda_q2, lambda_k2, diff_norm_g, w_mem_kv, w_out, b_out,
           ln1_g, ln1_b, w_up, w_down, ln2_g, ln2_b):
    bsz, s, d = x.shape
    conv_ch = conv_w.shape[2]
    dh = lambda_q1.shape[1]
    qk_w = DIFF_HEADS * 2 * dh
    diff_w = DIFF_HEADS * diff_norm_g.shape[1]
    mem_w = w_mem_kv.shape[2] // 2
    dims = (conv_ch, qk_w, diff_w, mem_w)
    assert s % (QT_PER_STEP * ATT_TILE) == 0 and (s // ATT_TILE) % PV_CHUNK == 0 and 2 * dh == LANES and diff_norm_g.shape[1] == LANES

    xs = _embed_ln(x.reshape(bsz * s, d), emb_ln_g, emb_ln_b)
    bias_tiles, far = _bias_tiles(rel_bias, s)
    for l in range(DEPTH):
        lam_init = 0.8 - 0.6 * math.exp(-0.3 * l)
        conv_out, k, qt, vt, qm = _inproj_conv(xs.reshape(bsz, s, d), w_in[l], b_in[l], conv_w[l], conv_b[l],
                                               conv_ln_g[l], conv_ln_b[l], dims)
        diff_out = _diffattn(qt, k, vt, bias_tiles, far, rel_bias, lam_init, diff_norm_g[l],
                             lambda_q1[l], lambda_k1[l], lambda_q2[l], lambda_k2[l])
        kbd, vbd = _memkv(mem, w_mem_kv[l], mem_w)
        xs = _outproj_ln(xs.reshape(bsz, s, d), conv_out, diff_out, qm, kbd, vbd,
                         w_out[l], b_out[l], ln1_g[l], ln1_b[l])
        xs = _mlp_ln(xs.reshape(bsz * s, d), w_up[l], w_down[l], ln2_g[l], ln2_b[l])
    return xs.reshape(bsz, s, d)
```

```python
import functools
import math

import jax
import jax.numpy as jnp
from jax import lax
from jax.experimental import pallas as pl
from jax.experimental.pallas import tpu as pltpu

F32 = jnp.float32
BF16 = jnp.bfloat16

DEPTH = 4
CONV_WIDTH = 31
CONV_PAD = (CONV_WIDTH - 1) // 2
DIFF_HEADS = 4
MEM_HEADS = 4
N_BUCKETS = 32
MAX_DISTANCE = 128
ALPHA = (2.0 * DEPTH) ** 0.25
LN_EPS = 1e-5
LOG2E = math.log2(math.e)

LANES = 128
SUBLANES = 8
BF16_ROWS = 16
MXU_DIM = 256
VMEM_LIMIT = 56 * 1024 * 1024

ATT_TILE = MXU_DIM
QT_PER_STEP = 2
PAIRS_PER_STEP = 2
PV_CHUNK = 4
HALO = 16

_NT = (((1,), (1,)), ((), ()))


def _cparams(*sem):
    return pltpu.CompilerParams(dimension_semantics=sem, vmem_limit_bytes=VMEM_LIMIT)


def _layer_norm(z, g, b):
    mu = jnp.mean(z, axis=-1, keepdims=True)
    zc = z - mu
    var = jnp.mean(zc * zc, axis=-1, keepdims=True)
    return zc * lax.rsqrt(var + LN_EPS) * g + b


def _ln_kernel(x_ref, g_ref, b_ref, o_ref):
    o_ref[...] = _layer_norm(x_ref[...], g_ref[...], b_ref[...])


def _embed_ln(x2d, g, b, tm=1024):
    m, d = x2d.shape
    return pl.pallas_call(
        _ln_kernel,
        grid=(m // tm,),
        in_specs=[pl.BlockSpec((tm, d), lambda i: (i, 0)),
                  pl.BlockSpec((1, d), lambda i: (0, 0)),
                  pl.BlockSpec((1, d), lambda i: (0, 0))],
        out_specs=pl.BlockSpec((tm, d), lambda i: (i, 0)),
        out_shape=jax.ShapeDtypeStruct((m, d), F32),
        compiler_params=_cparams("parallel"),
        name="embed_ln",
    )(x2d, g.reshape(1, d), b.reshape(1, d))


def _inproj_conv_kernel(x_ref, xp_ref, xn_ref, wc_ref, wk_ref, wqt_ref, wvt_ref, wm_ref,
                        bc_ref, bk_ref, bqt_ref, bvt_ref, bm_ref, cw_ref, cb_ref, cg_ref, cbeta_ref,
                        conv_ref, k_ref, qt_ref, vt_ref, qm_ref, u_ref, us_ref, *, q_scale, qm_scale, ch, rows):
    i = pl.program_id(1)
    n = pl.num_programs(1)
    tm = x_ref.shape[1]
    xb = x_ref[0].astype(BF16)

    def glu(c):
        return c[:, :ch] * jax.nn.sigmoid(c[:, ch:])

    halo = jnp.concatenate([xp_ref[0], xn_ref[0]], axis=0).astype(BF16)
    c_halo = jnp.dot(halo, wc_ref[...], preferred_element_type=F32) + bc_ref[...]
    c_cur = jnp.dot(xb, wc_ref[...], preferred_element_type=F32) + bc_ref[...]
    u_ref[0:HALO, :] = jnp.where(i > 0, glu(c_halo[:HALO]), 0.0)
    u_ref[HALO:HALO + tm, :] = glu(c_cur)
    u_ref[HALO + tm:, :] = jnp.where(i < n - 1, glu(c_halo[HALO:]), 0.0)
    for ph in range(SUBLANES):
        us_ref[ph] = u_ref[ph:ph + us_ref.shape[1], :]

    k_ref[0] = (jnp.dot(xb, wk_ref[...], preferred_element_type=F32) + bk_ref[...]).astype(BF16)
    qt = lax.dot_general(wqt_ref[...], xb, _NT, preferred_element_type=F32) + bqt_ref[...]
    qt_ref[0, 0] = (qt * q_scale).astype(BF16)
    vt = lax.dot_general(wvt_ref[...], xb, _NT, preferred_element_type=F32) + bvt_ref[...]
    vt_ref[0] = vt.astype(BF16)
    qm = jnp.dot(xb, wm_ref[...], preferred_element_type=F32) + bm_ref[...]
    qm_ref[0] = (qm * qm_scale).astype(BF16)

    w = cw_ref[...]
    for r in range(0, tm, rows):
        acc = jnp.zeros((rows, ch), F32)
        for t in range(CONV_WIDTH):
            tiles, ph = divmod(HALO - CONV_PAD + t, SUBLANES)
            start = r + tiles * SUBLANES
            acc = acc + us_ref[ph, start:start + rows, :] * w[t:t + 1, :]
        y = _layer_norm(acc + cb_ref[...], cg_ref[...], cbeta_ref[...])
        conv_ref[0, r:r + rows, :] = (y * jax.nn.sigmoid(y)).astype(BF16)


def _inproj_conv(x, w_in, b_in, conv_w, conv_b, conv_ln_g, conv_ln_b, dims, tm=512, rows=64):
    bsz, s, d = x.shape
    conv_ch, qk_w, diff_w, mem_w = dims
    s0 = 2 * conv_ch
    s1 = s0 + qk_w
    s2 = s1 + qk_w
    s3 = s2 + diff_w
    wb = w_in.astype(BF16)
    wc, wq, wk, wv, wm = wb[:, :s0], wb[:, s0:s1], wb[:, s1:s2], wb[:, s2:s3], wb[:, s3:]
    bc, bq, bk, bv, bm = b_in[:s0], b_in[s0:s1], b_in[s1:s2], b_in[s2:s3], b_in[s3:]
    dh = qk_w // (2 * DIFF_HEADS)
    mem_dh = mem_w // MEM_HEADS
    hb = tm // HALO
    nh = s // HALO
    kern = functools.partial(_inproj_conv_kernel, q_scale=dh ** -0.5 * LOG2E, qm_scale=mem_dh ** -0.5,
                             ch=conv_ch, rows=rows)
    const = lambda shape: pl.BlockSpec(shape, lambda b, i: (0, 0))
    return pl.pallas_call(
        kern,
        grid=(bsz, s // tm),
        in_specs=[pl.BlockSpec((1, tm, d), lambda b, i: (b, i, 0)),
                  pl.BlockSpec((1, HALO, d), lambda b, i: (b, jnp.maximum(i * hb - 1, 0), 0)),
                  pl.BlockSpec((1, HALO, d), lambda b, i: (b, jnp.minimum((i + 1) * hb, nh - 1), 0)),
                  const((d, s0)), const((d, qk_w)), const((qk_w, d)), const((diff_w, d)), const((d, mem_w)),
                  const((1, s0)), const((1, qk_w)), const((qk_w, 1)), const((diff_w, 1)), const((1, mem_w)),
                  const((CONV_WIDTH, conv_ch)), const((1, conv_ch)), const((1, conv_ch)), const((1, conv_ch))],
        out_specs=[pl.BlockSpec((1, tm, conv_ch), lambda b, i: (b, i, 0)),
                   pl.BlockSpec((1, tm, qk_w), lambda b, i: (b, i, 0)),
                   pl.BlockSpec((1, 1, qk_w, tm), lambda b, i: (b, i, 0, 0)),
                   pl.BlockSpec((1, diff_w, tm), lambda b, i: (b, 0, i)),
                   pl.BlockSpec((1, tm, mem_w), lambda b, i: (b, i, 0))],
        out_shape=[jax.ShapeDtypeStruct((bsz, s, conv_ch), BF16),
                   jax.ShapeDtypeStruct((bsz, s, qk_w), BF16),
                   jax.ShapeDtypeStruct((bsz, s // tm, qk_w, tm), BF16),
                   jax.ShapeDtypeStruct((bsz, diff_w, s), BF16),
                   jax.ShapeDtypeStruct((bsz, s, mem_w), BF16)],
        scratch_shapes=[pltpu.VMEM((tm + 2 * HALO, conv_ch), F32),
                        pltpu.VMEM((SUBLANES, tm + 2 * HALO - SUBLANES, conv_ch), F32)],
        compiler_params=_cparams("parallel", "parallel"),
        name="inproj_conv",
    )(x, x, x, wc, wk, wq.T, wv.T, wm,
      bc.reshape(1, -1), bk.reshape(1, -1), bq.reshape(-1, 1), bv.reshape(-1, 1), bm.reshape(1, -1),
      conv_w, conv_b.reshape(1, -1), conv_ln_g.reshape(1, -1), conv_ln_b.reshape(1, -1))


def _memkv_kernel(mem_ref, wkt_ref, wv_ref, kbd_ref, vbd_ref, *, heads):
    mb = mem_ref[0].astype(BF16)
    kmt = lax.dot_general(wkt_ref[...], mb, _NT, preferred_element_type=F32)
    vm = jnp.dot(mb, wv_ref[...], preferred_element_type=F32)
    w, m = kmt.shape
    dh = w // heads
    row_head = lax.broadcasted_iota(jnp.int32, (w, m), 0) // dh
    col_head = lax.broadcasted_iota(jnp.int32, (m, w), 1) // dh
    for h in range(heads):
        kbd_ref[0, :, h * m:(h + 1) * m] = jnp.where(row_head == h, kmt, 0.0).astype(BF16)
        vbd_ref[0, h * m:(h + 1) * m, :] = jnp.where(col_head == h, vm, 0.0).astype(BF16)


def _memkv(mem, w_mem_kv, mem_w):
    bsz, m, d = mem.shape
    wb = w_mem_kv.astype(BF16)
    wkt, wv = wb[:, :mem_w].T, wb[:, mem_w:]
    kern = functools.partial(_memkv_kernel, heads=MEM_HEADS)
    return pl.pallas_call(
        kern,
        grid=(bsz,),
        in_specs=[pl.BlockSpec((1, m, d), lambda b: (b, 0, 0)),
                  pl.BlockSpec((mem_w, d), lambda b: (0, 0)),
                  pl.BlockSpec((d, mem_w), lambda b: (0, 0))],
        out_specs=[pl.BlockSpec((1, mem_w, MEM_HEADS * m), lambda b: (b, 0, 0)),
                   pl.BlockSpec((1, MEM_HEADS * m, mem_w), lambda b: (b, 0, 0))],
        out_shape=[jax.ShapeDtypeStruct((bsz, mem_w, MEM_HEADS * m), BF16),
                   jax.ShapeDtypeStruct((bsz, MEM_HEADS * m, mem_w), BF16)],
        compiler_params=_cparams("parallel"),
        name="mem_kv",
    )(mem, wkt, wv)


def _memattn_rows(qm, kbd, vbd, heads):
    s = jnp.dot(qm, kbd, preferred_element_type=F32)
    m = s.shape[1] // heads
    w = vbd.shape[1]
    dh = w // heads
    lane_head = lax.broadcasted_iota(jnp.int32, (1, w), 1) // dh
    ps = []
    scale = jnp.zeros((s.shape[0], w), F32)
    for h in range(heads):
        sh = s[:, h * m:(h + 1) * m]
        p = jnp.exp(sh - jnp.max(sh, axis=-1, keepdims=True))
        scale = jnp.where(lane_head == h, 1.0 / jnp.sum(p, axis=-1, keepdims=True), scale)
        ps.append(p.astype(BF16))
    o = jnp.dot(jnp.concatenate(ps, axis=1), vbd, preferred_element_type=F32)
    return (o * scale).astype(BF16)


def _t5_bucket(rel):
    half = N_BUCKETS // 2
    max_exact = half // 2
    ret = (rel > 0).astype(jnp.int32) * half
    n = jnp.abs(rel)
    nf = jnp.maximum(n, 1).astype(jnp.float32)
    large = max_exact + (jnp.log(nf / max_exact) / math.log(MAX_DISTANCE / max_exact)
                         * (half - max_exact)).astype(jnp.int32)
    large = jnp.minimum(large, half - 1)
    return ret + jnp.where(n < max_exact, n, large)


def _bias_tile_kernel(relb_ref, idx_ref, o_ref):
    h = pl.program_id(0)
    idx = idx_ref[0]
    acc = jnp.zeros(idx.shape, F32)
    for bkt in range(N_BUCKETS):
        acc = jnp.where(idx == bkt, relb_ref[bkt, h], acc)
    o_ref[0, 0] = acc * LOG2E


def _bias_tiles(rel_bias, s):
    t = ATT_TILE
    sat = (N_BUCKETS // 4) * (MAX_DISTANCE / (N_BUCKETS // 4)) ** ((N_BUCKETS // 2 - 1 - N_BUCKETS // 4) / (N_BUCKETS // 4))
    assert t + 1 > sat + 1, "attention tile too small for the saturated-bias shortcut"
    kk = jnp.arange(t, dtype=jnp.int32)[:, None]
    qq = jnp.arange(t, dtype=jnp.int32)[None, :]
    idx = jnp.stack([_t5_bucket((d - 1) * t + kk - qq) for d in range(3)])
    far = _t5_bucket(jnp.array([-(s - 1), s - 1], jnp.int32))
    tiles = pl.pallas_call(
        _bias_tile_kernel,
        grid=(DIFF_HEADS, 3),
        in_specs=[pl.BlockSpec(memory_space=pltpu.SMEM),
                  pl.BlockSpec((1, t, t), lambda h, d: (d, 0, 0))],
        out_specs=pl.BlockSpec((1, 1, t, t), lambda h, d: (h, d, 0, 0)),
        out_shape=jax.ShapeDtypeStruct((DIFF_HEADS, 3, t, t), F32),
        compiler_params=_cparams("parallel", "parallel"),
        name="bias_tiles",
    )(rel_bias, idx)
    return tiles, far


def _diffattn_kernel(*refs, dh, dv, nk):
    step = pl.program_id(2)

    def body(pr, carry):
        _diffattn_group(step * PAIRS_PER_STEP + pr, pr, *refs, dh=dh, dv=dv, nk=nk)
        return carry

    lax.fori_loop(0, PAIRS_PER_STEP, body, 0)


def _diffattn_group(pair, pr, far_ref, relb_ref, lami_ref, qt_ref, k_ref, vt_ref, bias_ref, g_ref,
                    lq1_ref, lk1_ref, lq2_ref, lk2_ref, o_ref, vaug_ref, s_ref, p_ref, *, dh, dv, nk):
    t = ATT_TILE
    h = pl.program_id(1)

    @pl.when(pair == 0)
    def _():
        vaug_ref[0:dv, :] = vt_ref[0]
        vaug_ref[dv:, :] = jnp.ones((BF16_ROWS, vaug_ref.shape[1]), BF16)

    c_left = relb_ref[far_ref[0], h] * LOG2E
    c_right = relb_ref[far_ref[1], h] * LOG2E
    row = lax.broadcasted_iota(jnp.int32, (2 * dh, t), 0)
    qis, qtms = [], []
    for u in range(QT_PER_STEP):
        qt = qt_ref[0, pr, :, u * t:(u + 1) * t]
        qis.append(pair * QT_PER_STEP + u)
        qtms.append((jnp.where(row < dh, qt, jnp.zeros_like(qt)), jnp.where(row >= dh, qt, jnp.zeros_like(qt))))
    m8 = [[jnp.full((SUBLANES, t), -jnp.inf, F32) for _ in range(2)] for _ in range(QT_PER_STEP)]
    mrow = [[None, None] for _ in range(QT_PER_STEP)]

    def colmax(s):
        return jnp.max(s.reshape(t // SUBLANES, SUBLANES, t), axis=0)

    def far_bias(u, j, near):
        return jnp.where(j < qis[u] - 1, c_left, jnp.where(j > qis[u] + 1, c_right, near))

    def score_tile(u, j):
        c_far = far_bias(u, j, -jnp.inf)
        for mp in range(2):
            s = jnp.dot(k_ref[0, j * t:(j + 1) * t, :], qtms[u][mp], preferred_element_type=F32)
            s_ref[u % 2, mp, j * t:(j + 1) * t, :] = s
            m8[u][mp] = jnp.maximum(m8[u][mp], colmax(s) + c_far)

    def finish_scores(u):
        for d in range(3):
            j = qis[u] - 1 + d
            valid = jnp.logical_and(j >= 0, j < nk)
            rows = pl.ds(pl.multiple_of(jnp.clip(j, 0, nk - 1) * t, t), t)
            bias = jnp.where(valid, bias_ref[0, d], 0.0)
            for mp in range(2):
                s = s_ref[u % 2, mp, rows, :] + bias
                s_ref[u % 2, mp, rows, :] = s
                m8[u][mp] = jnp.maximum(m8[u][mp], jnp.where(valid, colmax(s), -jnp.inf))
        for mp in range(2):
            mrow[u][mp] = jnp.max(m8[u][mp], axis=0, keepdims=True)

    def exp_tile(u, j):
        c = far_bias(u, j, 0.0)
        for mp in range(2):
            p = jnp.exp2(s_ref[u % 2, mp, j * t:(j + 1) * t, :] - (mrow[u][mp] - c))
            p_ref[u % 2, mp, j * t:(j + 1) * t, :] = p.astype(BF16)

    def value_chunk(u, c, acc):
        keys = slice(c * PV_CHUNK * t, (c + 1) * PV_CHUNK * t)
        out = []
        for mp in range(2):
            part = jnp.dot(vaug_ref[:, keys], p_ref[u % 2, mp, keys, :], preferred_element_type=F32)
            out.append(part if acc is None else acc[mp] + part)
        return out

    def finish(u, acc):
        o1 = acc[0][0:dv] / acc[0][dv:dv + 1]
        o2 = acc[1][0:dv] / acc[1][dv:dv + 1]
        lam_init = lami_ref[0]
        e1 = jnp.exp(jnp.sum(lq1_ref[...] * lk1_ref[...], axis=-1, keepdims=True))
        e2 = jnp.exp(jnp.sum(lq2_ref[...] * lk2_ref[...], axis=-1, keepdims=True))
        lam = e1 - e2 + lam_init
        ot = o1 - lam * o2
        ms = jnp.mean(ot * ot, axis=0, keepdims=True)
        ot = ot * lax.rsqrt(ms + LN_EPS) * g_ref[...] * (1.0 - lam_init)
        o_ref[0, pl.ds(pl.multiple_of((pr * QT_PER_STEP + u) * t, t), t), :] = ot.T.astype(BF16)

    for ph in range(QT_PER_STEP + 2):
        acc = None
        for j in range(nk):
            if 1 <= ph <= QT_PER_STEP:
                exp_tile(ph - 1, j)
            if ph < QT_PER_STEP:
                score_tile(ph, j)
            if ph >= 2 and j % PV_CHUNK == 0:
                acc = value_chunk(ph - 2, j // PV_CHUNK, acc)
        if ph < QT_PER_STEP:
            finish_scores(ph)
        if ph >= 2:
            finish(ph - 2, acc)


def _diffattn(qt, k, vt, bias_tiles, far, rel_bias, lam_init, norm_g, lq1, lk1, lq2, lk2):
    bsz, s, qk_w = k.shape
    diff_w = vt.shape[1]
    dv = diff_w // DIFF_HEADS
    dh = qk_w // (2 * DIFF_HEADS)
    t = ATT_TILE
    nk = s // t
    tq = QT_PER_STEP * t
    tstep = PAIRS_PER_STEP * tq
    assert qt.shape[3] == tq, "query slabs must hold QT_PER_STEP query tiles"
    kern = functools.partial(_diffattn_kernel, dh=dh, dv=dv, nk=nk)
    smem = pl.BlockSpec(memory_space=pltpu.SMEM)
    lvec = lambda: pl.BlockSpec((1, dh), lambda b, h, i: (0, 0))
    return pl.pallas_call(
        kern,
        grid=(bsz, DIFF_HEADS, s // tstep),
        in_specs=[smem, smem, smem,
                  pl.BlockSpec((1, PAIRS_PER_STEP, 2 * dh, tq), lambda b, h, i: (b, i, h, 0)),
                  pl.BlockSpec((1, s, 2 * dh), lambda b, h, i: (b, 0, h)),
                  pl.BlockSpec((1, dv, s), lambda b, h, i: (b, h, 0)),
                  pl.BlockSpec((1, 3, t, t), lambda b, h, i: (h, 0, 0, 0)),
                  pl.BlockSpec((dv, 1), lambda b, h, i: (0, 0)),
                  lvec(), lvec(), lvec(), lvec()],
        out_specs=pl.BlockSpec((1, tstep, dv), lambda b, h, i: (b, i, h)),
        out_shape=jax.ShapeDtypeStruct((bsz, s, diff_w), BF16),
        scratch_shapes=[pltpu.VMEM((dv + BF16_ROWS, s), BF16),
                        pltpu.VMEM((2, 2, s, t), F32),
                        pltpu.VMEM((2, 2, s, t), BF16)],
        compiler_params=_cparams("parallel", "parallel", "arbitrary"),
        name="diff_attn",
    )(far, rel_bias, jnp.full((1,), lam_init, F32), qt, k, vt, bias_tiles, norm_g.reshape(dv, 1),
      lq1.reshape(1, dh), lk1.reshape(1, dh), lq2.reshape(1, dh), lk2.reshape(1, dh))


def _outproj_kernel(x_ref, c_ref, d_ref, qm_ref, kbd_ref, vbd_ref, w_ref, b_ref, g_ref, beta_ref, o_ref, *, rows):
    for r in range(0, x_ref.shape[1], rows):
        sl = slice(r, r + rows)
        mem_out = _memattn_rows(qm_ref[0, sl, :], kbd_ref[0], vbd_ref[0], MEM_HEADS)
        mixed = jnp.concatenate([c_ref[0, sl, :], d_ref[0, sl, :], mem_out], axis=1)
        y = jnp.dot(mixed, w_ref[...], preferred_element_type=F32)
        z = ALPHA * x_ref[0, sl, :] + (y + b_ref[...])
        o_ref[0, sl, :] = _layer_norm(z, g_ref[...], beta_ref[...])


def _outproj_ln(x, conv_out, diff_out, qm, kbd, vbd, w_out, b_out, g, beta, tm=512, rows=256):
    bsz, s, d = x.shape
    cw, dw, mw = conv_out.shape[2], diff_out.shape[2], qm.shape[2]
    hm = kbd.shape[2]
    kern = functools.partial(_outproj_kernel, rows=rows)
    rowblk = lambda w: pl.BlockSpec((1, tm, w), lambda b, i: (b, i, 0))
    const = lambda shape: pl.BlockSpec(shape, lambda b, i: (0, 0))
    return pl.pallas_call(
        kern,
        grid=(bsz, s // tm),
        in_specs=[rowblk(d), rowblk(cw), rowblk(dw), rowblk(mw),
                  pl.BlockSpec((1, mw, hm), lambda b, i: (b, 0, 0)),
                  pl.BlockSpec((1, hm, mw), lambda b, i: (b, 0, 0)),
                  const((cw + dw + mw, d)),
                  const((1, d)), const((1, d)), const((1, d))],
        out_specs=rowblk(d),
        out_shape=jax.ShapeDtypeStruct((bsz, s, d), F32),
        compiler_params=_cparams("parallel", "parallel"),
        name="outproj_ln",
    )(x, conv_out, diff_out, qm, kbd, vbd, w_out.astype(BF16),
      b_out.reshape(1, d), g.reshape(1, d), beta.reshape(1, d))


def _mlp_kernel(x_ref, wu_ref, wd_ref, g_ref, beta_ref, o_ref, *, chunk, rows):
    for r in range(0, x_ref.shape[0], rows):
        x = x_ref[r:r + rows, :]
        xb = x.astype(BF16)
        ff = jnp.zeros(x.shape, F32)
        for c in range(0, wu_ref.shape[1], chunk):
            hcol = jnp.maximum(jnp.dot(xb, wu_ref[:, c:c + chunk], preferred_element_type=F32), 0.0)
            ff = ff + jnp.dot((hcol * hcol).astype(BF16), wd_ref[c:c + chunk, :], preferred_element_type=F32)
        o_ref[r:r + rows, :] = _layer_norm(ALPHA * x + ff, g_ref[...], beta_ref[...])


def _mlp_ln(x2d, w_up, w_down, g, beta, tm=512, chunk=1024, rows=256):
    m, d = x2d.shape
    dff = w_up.shape[1]
    kern = functools.partial(_mlp_kernel, chunk=chunk, rows=rows)
    const = lambda shape: pl.BlockSpec(shape, lambda i: (0, 0))
    return pl.pallas_call(
        kern,
        grid=(m // tm,),
        in_specs=[pl.BlockSpec((tm, d), lambda i: (i, 0)),
                  const((d, dff)), const((dff, d)), const((1, d)), const((1, d))],
        out_specs=pl.BlockSpec((tm, d), lambda i: (i, 0)),
        out_shape=jax.ShapeDtypeStruct((m, d), F32),
        compiler_params=_cparams("parallel"),
        name="mlp_ln",
    )(x2d, w_up.astype(BF16), w_down.astype(BF16), g.reshape(1, d), beta.reshape(1, d))


def kernel(x, mem, emb_ln_g, emb_ln_b, rel_bias, w_in, b_in, conv_w, conv_b, conv_ln_g, conv_ln_b,
           lambda_q1, lambda_k1, lambda_q2, lambda_k2, diff_norm_g, w_mem_kv, w_out, b_out,
           ln1_g, ln1_b, w_up, w_down, ln2_g, ln2_b):
    bsz, s, d = x.shape
    conv_ch = conv_w.shape[2]
    dh = lambda_q1.shape[1]
    qk_w = DIFF_HEADS * 2 * dh
    diff_w = DIFF_HEADS * diff_norm_g.shape[1]
    mem_w = w_mem_kv.shape[2] // 2
    dims = (conv_ch, qk_w, diff_w, mem_w)
    assert s % (PAIRS_PER_STEP * QT_PER_STEP * ATT_TILE) == 0 and (s // ATT_TILE) % PV_CHUNK == 0 and 2 * dh == LANES and diff_norm_g.shape[1] == LANES

    xs = _embed_ln(x.reshape(bsz * s, d), emb_ln_g, emb_ln_b)
    bias_tiles, far = _bias_tiles(rel_bias, s)
    for l in range(DEPTH):
        lam_init = 0.8 - 0.6 * math.exp(-0.3 * l)
        conv_out, k, qt, vt, qm = _inproj_conv(xs.reshape(bsz, s, d), w_in[l], b_in[l], conv_w[l], conv_b[l],
                                               conv_ln_g[l], conv_ln_b[l], dims)
        diff_out = _diffattn(qt, k, vt, bias_tiles, far, rel_bias, lam_init, diff_norm_g[l],
                             lambda_q1[l], lambda_k1[l], lambda_q2[l], lambda_k2[l])
        kbd, vbd = _memkv(mem, w_mem_kv[l], mem_w)
        xs = _outproj_ln(xs.reshape(bsz, s, d), conv_out, diff_out, qm, kbd, vbd,
                         w_out[l], b_out[l], ln1_g[l], ln1_b[l])
        xs = _mlp_ln(xs.reshape(bsz * s, d), w_up[l], w_down[l], ln2_g[l], ln2_b[l])
    return xs.reshape(bsz, s, d)
```
